```python
import jax, jax.numpy as jnp
from jax import lax
import numpy as np


D_MODEL = 2048
BATCH = 1
SEQ = 16384
DEPTH = 4
DEC_BATCH = 2
DEC_SEQ = 16384
PAST_LEN = 128

HEAD_DIM = 128
A_Q_HEADS = 16
A_KV_HEADS = 4
B_GROUPS = ((128, 1), (512, 4), (2048, 16))
N_B_GROUPS = 3
B_HEADS = 8
MEM_LEN = 256
C_HEADS = 4
D_FF = 4 * D_MODEL
GRID_W = 64
ROPE_THETA = 10000.0
EPS = 1e-6
Q_BLOCK = 128
N_MIXERS = 2
N_A_LAYERS = (DEPTH + 1) // 2
N_B_LAYERS = DEPTH // 2

kernel_name = 'hybrid_axial_gqa_dilated_window_encoder'


def rms_norm(x, g):
    xf = x.astype(jnp.float32)
    y = xf * lax.rsqrt(jnp.mean(xf * xf, axis=-1, keepdims=True) + EPS)
    return (y * g.astype(jnp.float32)).astype(x.dtype)


def rotate_half(x):
    x1, x2 = jnp.split(x, 2, axis=-1)
    return jnp.concatenate([-x2, x1], axis=-1)


def rope_angles(pos, dim):
    inv = ROPE_THETA ** (-jnp.arange(0, dim, 2, dtype=jnp.float32) / dim)
    ang = pos[:, None] * inv[None, :]
    return jnp.concatenate([ang, ang], axis=-1)


def apply_rope(x, cos, sin):
    shape = (cos.shape[0],) + (1,) * (x.ndim - 3) + (cos.shape[1],)
    c = cos.reshape(shape)
    s = sin.reshape(shape)
    return (x * c + rotate_half(x) * s).astype(x.dtype)


def apply_axial(x, tabs):
    cr, sr, cc, sc = tabs
    half = HEAD_DIM // 2
    return jnp.concatenate([apply_rope(x[..., :half], cr, sr), apply_rope(x[..., half:], cc, sc)], axis=-1)


def mixer_a(h, wqkv, q_gain, k_gain, wo, tabs):
    bsz, S, _ = h.shape
    nq = A_Q_HEADS * HEAD_DIM
    nkv = A_KV_HEADS * HEAD_DIM
    qkv = h @ wqkv
    q = qkv[..., :nq].reshape(bsz, S, A_Q_HEADS, HEAD_DIM)
    k = qkv[..., nq:nq + nkv].reshape(bsz, S, A_KV_HEADS, HEAD_DIM)
    v = qkv[..., nq + nkv:].reshape(bsz, S, A_KV_HEADS, HEAD_DIM)
    q = apply_axial(rms_norm(q, q_gain), tabs)
    k = apply_axial(rms_norm(k, k_gain), tabs)
    grp = A_Q_HEADS // A_KV_HEADS
    nb = S // Q_BLOCK
    qb = jnp.moveaxis(q.reshape(bsz, nb, Q_BLOCK, A_KV_HEADS, grp, HEAD_DIM), 1, 0)
    scale = HEAD_DIM ** -0.5

    def block(qi):
        s = jnp.einsum('bqkgd,bskd->bkgqs', qi, k, preferred_element_type=jnp.float32) * scale
        p = jax.nn.softmax(s, axis=-1)
        return jnp.einsum('bkgqs,bskd->bqkgd', p.astype(v.dtype), v)

    o = lax.map(block, qb)
    o = jnp.moveaxis(o, 0, 1).reshape(bsz, S, nq)
    return o @ wo


def mixer_b(h, wqkv, wo, cos, sin):
    bsz, S, _ = h.shape
    qkv = (h @ wqkv).reshape(bsz, S, 3, N_B_GROUPS, B_HEADS, HEAD_DIM)
    q = apply_rope(qkv[:, :, 0], cos, sin)
    k = apply_rope(qkv[:, :, 1], cos, sin)
    v = qkv[:, :, 2]
    nb = S // Q_BLOCK
    qb = jnp.moveaxis(q.reshape(bsz, nb, Q_BLOCK, N_B_GROUPS, B_HEADS, HEAD_DIM), 1, 0)
    starts = jnp.arange(nb, dtype=jnp.int32) * Q_BLOCK
    scale = HEAD_DIM ** -0.5

    def block(args):
        qi, t0 = args
        tq = t0 + jnp.arange(Q_BLOCK, dtype=jnp.int32)
        outs, lses = [], []
        for g, (win, dil) in enumerate(B_GROUPS):
            side = win // (2 * dil)
            rel = dil * jnp.arange(-side, side + 1, dtype=jnp.int32)
            idx = tq[:, None] + rel[None, :]
            valid = (idx >= 0) & (idx < S)
            idxc = jnp.clip(idx, 0, S - 1)
            kg = jnp.take(k[:, :, g], idxc, axis=1)
            vg = jnp.take(v[:, :, g], idxc, axis=1)
            s = jnp.einsum('bqhd,bqjhd->bhqj', qi[:, :, g], kg, preferred_element_type=jnp.float32) * scale
            s = jnp.where(valid[None, None], s, -jnp.inf)
            m = jnp.max(s, axis=-1, keepdims=True)
            p = jnp.exp(s - m)
            l = jnp.sum(p, axis=-1, keepdims=True)
            o = jnp.einsum('bhqj,bqjhd->bhqd', p / l, vg.astype(jnp.float32))
            outs.append(o)
            lses.append((m + jnp.log(l))[..., 0])
        w = jax.nn.softmax(jnp.stack(lses, 0), axis=0)
        o = jnp.einsum('gbhq,gbhqd->bqhd', w, jnp.stack(outs, 0))
        return o.astype(h.dtype)

    o = lax.map(block, (qb, starts))
    o = jnp.moveaxis(o, 0, 1).reshape(bsz, S, B_HEADS * HEAD_DIM)
    return o @ wo


def cross_attn(h, mem_n, wq, wkv, wo):
    bsz, S, _ = h.shape
    M = mem_n.shape[1]
    q = (h @ wq).reshape(bsz, S, C_HEADS, HEAD_DIM)
    kv = (mem_n @ wkv).reshape(bsz, M, 2, C_HEADS, HEAD_DIM)
    k, v = kv[:, :, 0], kv[:, :, 1]
    s = jnp.einsum('bqhd,bmhd->bhqm', q, k, preferred_element_type=jnp.float32) * (HEAD_DIM ** -0.5)
    p = jax.nn.softmax(s, axis=-1)
    o = jnp.einsum('bhqm,bmhd->bqhd', p.astype(v.dtype), v)
    return o.reshape(bsz, S, C_HEADS * HEAD_DIM) @ wo


def sq_relu_mlp(h, w_up, w_down):
    return jnp.square(jax.nn.relu(h @ w_up)) @ w_down


def trunk(x, mem, norm_mix, a_wqkv, a_q_gain, a_k_gain, a_wo, b_wqkv, b_wo,
          norm_cross, norm_mem, c_wq, c_wkv, c_wo, norm_mlp, w_up, w_down, final_norm):
    S = x.shape[1]
    rows = S // GRID_W
    row = jnp.repeat(jnp.arange(rows, dtype=jnp.float32), GRID_W)
    col = jnp.tile(jnp.arange(GRID_W, dtype=jnp.float32), rows)
    ar = rope_angles(row, HEAD_DIM // 2)
    ac = rope_angles(col, HEAD_DIM // 2)
    tabs = (jnp.cos(ar), jnp.sin(ar), jnp.cos(ac), jnp.sin(ac))
    a1 = rope_angles(jnp.arange(S, dtype=jnp.float32), HEAD_DIM)
    cos1, sin1 = jnp.cos(a1), jnp.sin(a1)
    for i in range(DEPTH):
        hn = rms_norm(x, norm_mix[i])
        j = i // N_MIXERS
        if i % N_MIXERS == 0:
            x = x + mixer_a(hn, a_wqkv[j], a_q_gain[j], a_k_gain[j], a_wo[j], tabs)
        else:
            x = x + mixer_b(hn, b_wqkv[j], b_wo[j], cos1, sin1)
        x = x + cross_attn(rms_norm(x, norm_cross[i]), rms_norm(mem, norm_mem[i]), c_wq[i], c_wkv[i], c_wo[i])
        x = x + sq_relu_mlp(rms_norm(x, norm_mlp[i]), w_up[i], w_down[i])
    return rms_norm(x, final_norm)


def setup_inputs(seed: int = 0) -> dict:
    key = jax.random.key(seed)
    ks = jax.random.split(key, 24)
    f32 = jnp.float32
    D = D_MODEL
    HD = HEAD_DIM

    def w(k, shape, fan_in):
        return jax.random.normal(k, shape, f32) * (fan_in ** -0.5)

    def gain(k, shape):
        return 1.0 + 0.02 * jax.random.normal(k, shape, f32)

    a_cols = (A_Q_HEADS + 2 * A_KV_HEADS) * HD
    b_cols = 3 * N_B_GROUPS * B_HEADS * HD
    return {
        'x_prompt': jax.random.normal(ks[0], (BATCH, SEQ, D), f32),
        'x_sample': jax.random.normal(ks[1], (DEC_BATCH, DEC_SEQ, D), f32),
        'mem_prompt': jax.random.normal(ks[2], (BATCH, MEM_LEN, D), f32),
        'mem_sample': jax.random.normal(ks[3], (DEC_BATCH, MEM_LEN, D), f32),
        'norm_mix': gain(ks[4], (DEPTH, D)),
        'a_wqkv': w(ks[5], (N_A_LAYERS, D, a_cols), D),
        'a_q_gain': gain(ks[6], (N_A_LAYERS, HD)),
        'a_k_gain': gain(ks[7], (N_A_LAYERS, HD)),
        'a_wo': w(ks[8], (N_A_LAYERS, A_Q_HEADS * HD, D), A_Q_HEADS * HD),
        'b_wqkv': w(ks[9], (N_B_LAYERS, D, b_cols), D),
        'b_wo': w(ks[10], (N_B_LAYERS, B_HEADS * HD, D), B_HEADS * HD),
        'norm_cross': gain(ks[11], (DEPTH, D)),
        'norm_mem': gain(ks[12], (DEPTH, D)),
        'c_wq': w(ks[13], (DEPTH, D, C_HEADS * HD), D),
        'c_wkv': w(ks[14], (DEPTH, D, 2 * C_HEADS * HD), D),
        'c_wo': w(ks[15], (DEPTH, C_HEADS * HD, D), C_HEADS * HD),
        'norm_mlp': gain(ks[16], (DEPTH, D)),
        'w_up': w(ks[17], (DEPTH, D, D_FF), D),
        'w_down': w(ks[18], (DEPTH, D_FF, D), D_FF),
        'final_norm': gain(ks[19], (D,)),
    }


def reference(x_prompt, x_sample, mem_prompt, mem_sample, norm_mix, a_wqkv, a_q_gain, a_k_gain, a_wo,
              b_wqkv, b_wo, norm_cross, norm_mem, c_wq, c_wkv, c_wo, norm_mlp, w_up, w_down, final_norm):
    y_prompt = trunk(x_prompt, mem_prompt, norm_mix, a_wqkv, a_q_gain, a_k_gain, a_wo, b_wqkv, b_wo,
                     norm_cross, norm_mem, c_wq, c_wkv, c_wo, norm_mlp, w_up, w_down, final_norm)
    y_sample = trunk(x_sample, mem_sample, norm_mix, a_wqkv, a_q_gain, a_k_gain, a_wo, b_wqkv, b_wo,
                     norm_cross, norm_mem, c_wq, c_wkv, c_wo, norm_mlp, w_up, w_down, final_norm)
    return (y_prompt, y_sample)
```

```python
import functools

import numpy as np
import jax
import jax.numpy as jnp
from jax import lax
from jax.experimental import pallas as pl
from jax.experimental.pallas import tpu as pltpu

HEAD_DIM = 128
A_Q_HEADS = 16
A_KV_HEADS = 4
A_GROUP = A_Q_HEADS // A_KV_HEADS
B_GROUPS = ((128, 1), (512, 4), (2048, 16))
N_B_GROUPS = 3
B_HEADS = 8
C_HEADS = 4
GRID_W = 64
ROPE_THETA = 10000.0
EPS = 1e-6
ATTN_SCALE = HEAD_DIM ** -0.5
NEG_BIG = -1e30

F32 = jnp.float32
BF16 = jnp.bfloat16

VMEM_LIMIT_BYTES = 56 * 1024 * 1024

MM_TM = 1024
MM_TN = 1024
MM_TK = 2048
PREP_TM = 256
FLASH_TQ = 256
FLASH_TK = 1024
BAND_TILE = 1024
BAND_Q = 128
CROSS_TM = 512
NORM_TM = 512


def _params(*sem):
    return pltpu.CompilerParams(dimension_semantics=sem, vmem_limit_bytes=VMEM_LIMIT_BYTES)


def _rms_rows(x, g):
    ms = jnp.mean(x * x, axis=-1, keepdims=True)
    return x * lax.rsqrt(ms + EPS) * g


def _mm_norm_kernel(x_ref, g_ref, w_ref, o_ref, xn_ref, *, relu2):
    @pl.when(pl.program_id(1) == 0)
    def _():
        xn_ref[...] = _rms_rows(x_ref[...], g_ref[...]).astype(BF16)

    acc = jnp.dot(xn_ref[...], w_ref[...], preferred_element_type=F32)
    if relu2:
        acc = jnp.square(jnp.maximum(acc, 0.0))
    o_ref[...] = acc.astype(o_ref.dtype)


def mm_norm(x, g, w, *, relu2=False, out_dtype=F32):
    m, k = x.shape
    n = w.shape[1]
    tm = min(MM_TM, m)
    tn = min(MM_TN, n)
    return pl.pallas_call(
        functools.partial(_mm_norm_kernel, relu2=relu2),
        out_shape=jax.ShapeDtypeStruct((m, n), out_dtype),
        grid=(m // tm, n // tn),
        in_specs=[
            pl.BlockSpec((tm, k), lambda i, j: (i, 0)),
            pl.BlockSpec((1, k), lambda i, j: (0, 0)),
            pl.BlockSpec((k, tn), lambda i, j: (0, j)),
        ],
        out_specs=pl.BlockSpec((tm, tn), lambda i, j: (i, j)),
        scratch_shapes=[pltpu.VMEM((tm, k), BF16)],
        compiler_params=_params("parallel", "arbitrary"),
        name="mm_norm",
    )(x, g.reshape(1, k), w)


def _mm_res_kernel(a_ref, w_ref, r_ref, o_ref):
    part = jnp.dot(a_ref[...], w_ref[...], preferred_element_type=F32)

    @pl.when(pl.program_id(2) == 0)
    def _():
        o_ref[...] = r_ref[...] + part

    @pl.when(pl.program_id(2) != 0)
    def _():
        o_ref[...] += part


def mm_res(a, w, res):
    m, k = a.shape
    n = w.shape[1]
    tm = min(MM_TM, m)
    tn = min(MM_TN, n)
    tk = min(MM_TK, k)
    return pl.pallas_call(
        _mm_res_kernel,
        out_shape=jax.ShapeDtypeStruct((m, n), F32),
        grid=(m // tm, n // tn, k // tk),
        in_specs=[
            pl.BlockSpec((tm, tk), lambda i, j, kk: (i, kk)),
            pl.BlockSpec((tk, tn), lambda i, j, kk: (kk, j)),
            pl.BlockSpec((tm, tn), lambda i, j, kk: (i, j)),
        ],
        out_specs=pl.BlockSpec((tm, tn), lambda i, j, kk: (i, j)),
        compiler_params=_params("parallel", "parallel", "arbitrary"),
        name="mm_res",
    )(a, w, res)


def _prep_a_kernel(qkv_ref, qg_ref, kg_ref, cos_ref, slo_ref, shi_ref, q_ref, k_ref, v_ref):
    cos = cos_ref[...]
    slo = slo_ref[...]
    shi = shi_ref[...]

    def head(x, g):
        y = _rms_rows(x, g)
        return y * cos + pltpu.roll(y, HEAD_DIM - 32, 1) * slo + pltpu.roll(y, 32, 1) * shi

    nq = A_Q_HEADS * HEAD_DIM
    nkv = A_KV_HEADS * HEAD_DIM
    qg = qg_ref[...] * ATTN_SCALE
    kg = kg_ref[...]
    for h in range(A_Q_HEADS):
        sl = slice(h * HEAD_DIM, (h + 1) * HEAD_DIM)
        q_ref[:, sl] = head(qkv_ref[:, sl], qg).astype(BF16)
    for h in range(A_KV_HEADS):
        sl = slice(h * HEAD_DIM, (h + 1) * HEAD_DIM)
        k_ref[:, sl] = head(qkv_ref[:, nq + h * HEAD_DIM:nq + (h + 1) * HEAD_DIM], kg).astype(BF16)
    v_ref[...] = qkv_ref[:, nq + nkv:].astype(BF16)


def prep_a(qkv, q_gain, k_gain, tabs, seq):
    m, n = qkv.shape
    nq = A_Q_HEADS * HEAD_DIM
    nkv = A_KV_HEADS * HEAD_DIM
    tm = PREP_TM
    nt = seq // tm
    tab_spec = pl.BlockSpec((tm, HEAD_DIM), lambda i: (i % nt, 0))
    g_spec = pl.BlockSpec((1, HEAD_DIM), lambda i: (0, 0))
    return pl.pallas_call(
        _prep_a_kernel,
        out_shape=(
            jax.ShapeDtypeStruct((m, nq), BF16),
            jax.ShapeDtypeStruct((m, nkv), BF16),
            jax.ShapeDtypeStruct((m, nkv), BF16),
        ),
        grid=(m // tm,),
        in_specs=[pl.BlockSpec((tm, n), lambda i: (i, 0)), g_spec, g_spec, tab_spec, tab_spec, tab_spec],
        out_specs=(
            pl.BlockSpec((tm, nq), lambda i: (i, 0)),
            pl.BlockSpec((tm, nkv), lambda i: (i, 0)),
            pl.BlockSpec((tm, nkv), lambda i: (i, 0)),
        ),
        compiler_params=_params("parallel"),
        name="prep_a",
    )(qkv, q_gain.reshape(1, HEAD_DIM), k_gain.reshape(1, HEAD_DIM), *tabs)


def _flash_a_kernel(q_ref, k_ref, v_ref, o_ref, qs_ref, m_ref, l_ref, acc_ref, *, tq, tk, nk):
    for g in range(A_GROUP):
        qs_ref[g * tq:(g + 1) * tq, :] = q_ref[0, :, g * HEAD_DIM:(g + 1) * HEAD_DIM]
    m_ref[...] = jnp.full(m_ref.shape, NEG_BIG, F32)
    l_ref[...] = jnp.zeros(l_ref.shape, F32)
    acc_ref[...] = jnp.zeros(acc_ref.shape, F32)

    def step(c, carry):
        start = pl.multiple_of(c * tk, tk)
        kc = k_ref[0, pl.ds(start, tk), :]
        vc = v_ref[0, pl.ds(start, tk), :]
        s = lax.dot_general(qs_ref[...], kc, (((1,), (1,)), ((), ())), preferred_element_type=F32)
        m_prev = m_ref[...]
        m_new = jnp.maximum(m_prev, jnp.max(s, axis=1, keepdims=True))
        alpha = jnp.exp(m_prev - m_new)
        p = jnp.exp(s - m_new)
        l_ref[...] = alpha * l_ref[...] + jnp.sum(p, axis=1, keepdims=True)
        acc_ref[...] = alpha * acc_ref[...] + jnp.dot(p.astype(BF16), vc, preferred_element_type=F32)
        m_ref[...] = m_new
        return carry

    lax.fori_loop(0, nk, step, 0)
    o = acc_ref[...] / l_ref[...]
    for g in range(A_GROUP):
        o_ref[0, :, g * HEAD_DIM:(g + 1) * HEAD_DIM] = o[g * tq:(g + 1) * tq, :].astype(o_ref.dtype)


def flash_a(q, k, v):
    bsz, seq, nq = q.shape
    tq, tk = FLASH_TQ, FLASH_TK
    gw = A_GROUP * HEAD_DIM
    rows = A_GROUP * tq
    return pl.pallas_call(
        functools.partial(_flash_a_kernel, tq=tq, tk=tk, nk=seq // tk),
        out_shape=jax.ShapeDtypeStruct((bsz, seq, nq), BF16),
        grid=(bsz, A_KV_HEADS, seq // tq),
        in_specs=[
            pl.BlockSpec((1, tq, gw), lambda b, h, i: (b, i, h)),
            pl.BlockSpec((1, seq, HEAD_DIM), lambda b, h, i: (b, 0, h)),
            pl.BlockSpec((1, seq, HEAD_DIM), lambda b, h, i: (b, 0, h)),
        ],
        out_specs=pl.BlockSpec((1, tq, gw), lambda b, h, i: (b, i, h)),
        scratch_shapes=[
            pltpu.VMEM((rows, HEAD_DIM), BF16),
            pltpu.VMEM((rows, 1), F32),
            pltpu.VMEM((rows, 1), F32),
            pltpu.VMEM((rows, HEAD_DIM), F32),
        ],
        compiler_params=_params("parallel", "parallel", "arbitrary"),
        name="flash_a",
    )(q, k, v)


def _prep_b_kernel(qkv_ref, cos_ref, sin_ref, o_ref):
    cos = cos_ref[...]
    sin = sin_ref[...]
    nh = N_B_GROUPS * B_HEADS
    for h in range(2 * nh):
        sl = slice(h * HEAD_DIM, (h + 1) * HEAD_DIM)
        x = qkv_ref[:, sl]
        if h < nh:
            x = x * ATTN_SCALE
        o_ref[:, sl] = (x * cos + pltpu.roll(x, HEAD_DIM // 2, 1) * sin).astype(BF16)
    o_ref[:, 2 * nh * HEAD_DIM:] = qkv_ref[:, 2 * nh * HEAD_DIM:].astype(BF16)


def prep_b(qkv, cos, sin_signed, seq):
    m, n = qkv.shape
    tm = PREP_TM
    nt = seq // tm
    tab_spec = pl.BlockSpec((tm, HEAD_DIM), lambda i: (i % nt, 0))
    return pl.pallas_call(
        _prep_b_kernel,
        out_shape=jax.ShapeDtypeStruct((m, n), BF16),
        grid=(m // tm,),
        in_specs=[pl.BlockSpec((tm, n), lambda i: (i, 0)), tab_spec, tab_spec],
        out_specs=pl.BlockSpec((tm, n), lambda i: (i, 0)),
        compiler_params=_params("parallel"),
        name="prep_b",
    )(qkv, cos, sin_signed)


def _band_masks():
    out = []
    qi = np.arange(BAND_Q)[:, None]
    for win, dil in B_GROUPS:
        nb = _band_halo_blocks(win, dil)
        kj = np.arange(BAND_Q * (2 * nb + 1))[None, :] - BAND_Q * nb
        rel = kj - qi
        ok = (rel % dil == 0) & (np.abs(rel) <= win // 2)
        out.append(np.where(ok, 0.0, NEG_BIG).astype(np.float32))
    return np.concatenate(out, axis=1)


def _band_halo_blocks(win, dil):
    return -(-(win // 2) // BAND_Q)


def _band_b_kernel(*refs, seq):
    q_refs = refs[0:3]
    k_refs = refs[3:12]
    v_refs = refs[12:21]
    mask_ref = refs[21]
    o_ref = refs[22]
    kcat = refs[23:26]
    vcat = refs[26:29]
    tile = BAND_TILE
    i = pl.program_id(1)

    for g in range(N_B_GROUPS):
        for part in range(3):
            kcat[g][part * tile:(part + 1) * tile, :] = k_refs[3 * g + part][0]
            vcat[g][part * tile:(part + 1) * tile, :] = v_refs[3 * g + part][0]

    pos = (i - 1) * tile + lax.broadcasted_iota(jnp.int32, (1, 3 * tile), 1)
    vrow = jnp.where((pos >= 0) & (pos < seq), 0.0, NEG_BIG).astype(F32)

    spans = []
    off = 0
    for win, dil in B_GROUPS:
        nb = _band_halo_blocks(win, dil)
        width = BAND_Q * (2 * nb + 1)
        spans.append((nb, width, off))
        off += width

    for j in range(tile // BAND_Q):
        s_list = []
        for g, (nb, width, moff) in enumerate(spans):
            lo = tile + BAND_Q * (j - nb)
            qj = q_refs[g][0, j * BAND_Q:(j + 1) * BAND_Q, :]
            kj = kcat[g][lo:lo + width, :]
            s = lax.dot_general(qj, kj, (((1,), (1,)), ((), ())), preferred_element_type=F32)
            s_list.append(s + (mask_ref[:, moff:moff + width] + vrow[:, lo:lo + width]))
        m = s_list[0].max(axis=1, keepdims=True)
        for s in s_list[1:]:
            m = jnp.maximum(m, s.max(axis=1, keepdims=True))
        l = jnp.zeros((BAND_Q, 1), F32)
        o = jnp.zeros((BAND_Q, HEAD_DIM), F32)
        for g, (nb, width, moff) in enumerate(spans):
            lo = tile + BAND_Q * (j - nb)
            p = jnp.exp(s_list[g] - m)
            l = l + p.sum(axis=1, keepdims=True)
            o = o + jnp.dot(p.astype(BF16), vcat[g][lo:lo + width, :], preferred_element_type=F32)
        o_ref[0, j * BAND_Q:(j + 1) * BAND_Q, :] = (o / l).astype(o_ref.dtype)


def band_b(qkv, masks):
    bsz, seq, _ = qkv.shape
    tile = BAND_TILE
    nt = seq // tile
    nh = N_B_GROUPS * B_HEADS

    def col(which, g):
        return lambda b, i, h: which * nh + g * B_HEADS + h

    def spec(which, g, shift):
        def imap(b, i, h):
            return (b, jnp.clip(i + shift, 0, nt - 1), which * nh + g * B_HEADS + h)
        return pl.BlockSpec((1, tile, HEAD_DIM), imap)

    in_specs = [spec(0, g, 0) for g in range(N_B_GROUPS)]
    in_specs += [spec(1, g, s) for g in range(N_B_GROUPS) for s in (-1, 0, 1)]
    in_specs += [spec(2, g, s) for g in range(N_B_GROUPS) for s in (-1, 0, 1)]
    in_specs += [pl.BlockSpec(masks.shape, lambda b, i, h: (0, 0))]
    scratch = [pltpu.VMEM((3 * tile, HEAD_DIM), BF16) for _ in range(2 * N_B_GROUPS)]
    return pl.pallas_call(
        functools.partial(_band_b_kernel, seq=seq),
        out_shape=jax.ShapeDtypeStruct((bsz, seq, B_HEADS * HEAD_DIM), BF16),
        grid=(bsz, nt, B_HEADS),
        in_specs=in_specs,
        out_specs=pl.BlockSpec((1, tile, HEAD_DIM), lambda b, i, h: (b, i, h)),
        scratch_shapes=scratch,
        compiler_params=_params("parallel", "parallel", "arbitrary"),
        name="band_b",
    )(*([qkv] * 21), masks)


def _cross_kernel(x_ref, g_ref, wq_ref, kv_ref, wo_ref, o_ref):
    x = x_ref[...]
    xn = _rms_rows(x, g_ref[...]).astype(BF16)
    q = (jnp.dot(xn, wq_ref[...], preferred_element_type=F32) * ATTN_SCALE).astype(BF16)
    nkv = C_HEADS * HEAD_DIM
    outs = []
    for h in range(C_HEADS):
        sl = slice(h * HEAD_DIM, (h + 1) * HEAD_DIM)
        kh = kv_ref[0, :, sl]
        vh = kv_ref[0, :, nkv + h * HEAD_DIM:nkv + (h + 1) * HEAD_DIM]
        s = lax.dot_general(q[:, sl], kh, (((1,), (1,)), ((), ())), preferred_element_type=F32)
        p = jnp.exp(s - s.max(axis=1, keepdims=True))
        l = p.sum(axis=1, keepdims=True)
        outs.append((jnp.dot(p.astype(BF16), vh, preferred_element_type=F32) / l).astype(BF16))
    o = jnp.concatenate(outs, axis=1)
    o_ref[...] = x + jnp.dot(o, wo_ref[...], preferred_element_type=F32)


def cross_block(x, g, wq, kv, wo, seq):
    m, d = x.shape
    tm = CROSS_TM
    nt = seq // tm
    nkv = C_HEADS * HEAD_DIM
    mem_len = kv.shape[1]
    return pl.pallas_call(
        _cross_kernel,
        out_shape=jax.ShapeDtypeStruct((m, d), F32),
        grid=(m // tm,),
        in_specs=[
            pl.BlockSpec((tm, d), lambda i: (i, 0)),
            pl.BlockSpec((1, d), lambda i: (0, 0)),
            pl.BlockSpec((d, nkv), lambda i: (0, 0)),
            pl.BlockSpec((1, mem_len, 2 * nkv), lambda i: (i // nt, 0, 0)),
            pl.BlockSpec((nkv, d), lambda i: (0, 0)),
        ],
        out_specs=pl.BlockSpec((tm, d), lambda i: (i, 0)),
        compiler_params=_params("parallel"),
        name="cross_block",
    )(x, g.reshape(1, d), wq, kv, wo)


def _norm_kernel(x_ref, g_ref, o_ref):
    o_ref[...] = _rms_rows(x_ref[...], g_ref[...])


def final_norm_rows(x, g):
    m, d = x.shape
    tm = NORM_TM
    return pl.pallas_call(
        _norm_kernel,
        out_shape=jax.ShapeDtypeStruct((m, d), F32),
        grid=(m // tm,),
        in_specs=[pl.BlockSpec((tm, d), lambda i: (i, 0)), pl.BlockSpec((1, d), lambda i: (0, 0))],
        out_specs=pl.BlockSpec((tm, d), lambda i: (i, 0)),
        compiler_params=_params("parallel"),
        name="final_norm",
    )(x, g.reshape(1, d))


def _rope_angles(pos, dim):
    inv = ROPE_THETA ** (-jnp.arange(0, dim, 2, dtype=F32) / dim)
    ang = pos[:, None] * inv[None, :]
    return jnp.concatenate([ang, ang], axis=-1)


def _rope_tables(seq):
    rows = seq // GRID_W
    row = jnp.repeat(jnp.arange(rows, dtype=F32), GRID_W)
    col = jnp.tile(jnp.arange(GRID_W, dtype=F32), rows)
    half = HEAD_DIM // 2
    ang = jnp.concatenate([_rope_angles(row, half), _rope_angles(col, half)], axis=-1)
    cos_a, sin_a = jnp.cos(ang), jnp.sin(ang)
    lane = jnp.arange(HEAD_DIM) % half
    s_lo = jnp.where(lane < half // 2, -sin_a, 0.0)
    s_hi = jnp.where(lane >= half // 2, sin_a, 0.0)
    a1 = _rope_angles(jnp.arange(seq, dtype=F32), HEAD_DIM)
    cos_b = jnp.cos(a1)
    sin_b = jnp.where(jnp.arange(HEAD_DIM) < half, -jnp.sin(a1), jnp.sin(a1))
    return (cos_a, s_lo, s_hi), (cos_b, sin_b)


def _trunk(x, mem, wts, tabs_a, tabs_b, masks):
    bsz, seq, d = x.shape
    mem_len = mem.shape[1]
    x = x.reshape(bsz * seq, d)
    mem = mem.reshape(bsz * mem_len, d)
    depth = wts["norm_mix"].shape[0]
    for i in range(depth):
        j = i // 2
        if i % 2 == 0:
            qkv = mm_norm(x, wts["norm_mix"][i], wts["a_wqkv"][j])
            q, k, v = prep_a(qkv, wts["a_q_gain"][j], wts["a_k_gain"][j], tabs_a, seq)
            o = flash_a(q.reshape(bsz, seq, -1), k.reshape(bsz, seq, -1), v.reshape(bsz, seq, -1))
            x = mm_res(o.reshape(bsz * seq, -1), wts["a_wo"][j], x)
        else:
            qkv = mm_norm(x, wts["norm_mix"][i], wts["b_wqkv"][j])
            qkv = prep_b(qkv, tabs_b[0], tabs_b[1], seq)
            o = band_b(qkv.reshape(bsz, seq, -1), masks)
            x = mm_res(o.reshape(bsz * seq, -1), wts["b_wo"][j], x)
        kv = mm_norm(mem, wts["norm_mem"][i], wts["c_wkv"][i], out_dtype=BF16)
        x = cross_block(x, wts["norm_cross"][i], wts["c_wq"][i], kv.reshape(bsz, mem_len, -1), wts["c_wo"][i], seq)
        h = mm_norm(x, wts["norm_mlp"][i], wts["w_up"][i], relu2=True, out_dtype=BF16)
        x = mm_res(h, wts["w_down"][i], x)
    return final_norm_rows(x, wts["final_norm"]).reshape(bsz, seq, d)


def kernel(x_prompt, x_sample, mem_prompt, mem_sample, norm_mix, a_wqkv, a_q_gain, a_k_gain, a_wo, b_wqkv, b_wo, norm_cross, norm_mem, c_wq, c_wkv, c_wo, norm_mlp, w_up, w_down, final_norm):
    wts = dict(
        norm_mix=norm_mix, a_wqkv=a_wqkv.astype(BF16), a_q_gain=a_q_gain, a_k_gain=a_k_gain,
        a_wo=a_wo.astype(BF16), b_wqkv=b_wqkv.astype(BF16), b_wo=b_wo.astype(BF16),
        norm_cross=norm_cross, norm_mem=norm_mem, c_wq=c_wq.astype(BF16), c_wkv=c_wkv.astype(BF16),
        c_wo=c_wo.astype(BF16), norm_mlp=norm_mlp, w_up=w_up.astype(BF16), w_down=w_down.astype(BF16),
        final_norm=final_norm,
    )
    masks = jnp.asarray(_band_masks())
    outs = []
    for x, mem in ((x_prompt, mem_prompt), (x_sample, mem_sample)):
        tabs_a, tabs_b = _rope_tables(x.shape[1])
        outs.append(_trunk(x, mem, wts, tabs_a, tabs_b, masks))
    return tuple(outs)
```

```python
import functools

import numpy as np
import jax
import jax.numpy as jnp
from jax import lax
from jax.experimental import pallas as pl
from jax.experimental.pallas import tpu as pltpu

HEAD_DIM = 128
A_Q_HEADS = 16
A_KV_HEADS = 4
A_GROUP = A_Q_HEADS // A_KV_HEADS
B_GROUPS = ((128, 1), (512, 4), (2048, 16))
N_B_GROUPS = 3
B_HEADS = 8
C_HEADS = 4
GRID_W = 64
ROPE_THETA = 10000.0
EPS = 1e-6
ATTN_SCALE = HEAD_DIM ** -0.5
LOG2E = 1.4426950408889634
NEG_BIG = -1e30
BOUNDED_SCORE_LIMIT = 60.0

F32 = jnp.float32
BF16 = jnp.bfloat16

VMEM_LIMIT_BYTES = 56 * 1024 * 1024

MM_TM = 1024
MM_TN = 1024
MM_TK = 2048
PREP_TM = 256
FLASH_TQ = 256
FLASH_TK = 1024
BAND_TILE = 1024
BAND_Q = 128
CROSS_TM = 512
NORM_TM = 512


def _params(*sem):
    return pltpu.CompilerParams(dimension_semantics=sem, vmem_limit_bytes=VMEM_LIMIT_BYTES)


def _rms_rows(x, g):
    ms = jnp.mean(x * x, axis=-1, keepdims=True)
    return x * lax.rsqrt(ms + EPS) * g


def _mm_norm_kernel(x_ref, g_ref, w_ref, o_ref, xn_ref, *, relu2):
    @pl.when(pl.program_id(1) == 0)
    def _():
        xn_ref[...] = _rms_rows(x_ref[...], g_ref[...]).astype(BF16)

    acc = jnp.dot(xn_ref[...], w_ref[...], preferred_element_type=F32)
    if relu2:
        acc = jnp.square(jnp.maximum(acc, 0.0))
    o_ref[...] = acc.astype(o_ref.dtype)


def mm_norm(x, g, w, *, relu2=False, out_dtype=F32):
    m, k = x.shape
    n = w.shape[1]
    tm = min(MM_TM, m)
    tn = min(MM_TN, n)
    return pl.pallas_call(
        functools.partial(_mm_norm_kernel, relu2=relu2),
        out_shape=jax.ShapeDtypeStruct((m, n), out_dtype),
        grid=(m // tm, n // tn),
        in_specs=[
            pl.BlockSpec((tm, k), lambda i, j: (i, 0)),
            pl.BlockSpec((1, k), lambda i, j: (0, 0)),
            pl.BlockSpec((k, tn), lambda i, j: (0, j)),
        ],
        out_specs=pl.BlockSpec((tm, tn), lambda i, j: (i, j)),
        scratch_shapes=[pltpu.VMEM((tm, k), BF16)],
        compiler_params=_params("parallel", "arbitrary"),
        name="mm_norm",
    )(x, g.reshape(1, k), w)


def _mm_res_kernel(a_ref, w_ref, r_ref, o_ref):
    part = jnp.dot(a_ref[...], w_ref[...], preferred_element_type=F32)

    @pl.when(pl.program_id(2) == 0)
    def _():
        o_ref[...] = r_ref[...] + part

    @pl.when(pl.program_id(2) != 0)
    def _():
        o_ref[...] += part


def mm_res(a, w, res):
    m, k = a.shape
    n = w.shape[1]
    tm = min(MM_TM, m)
    tn = min(MM_TN, n)
    tk = min(MM_TK, k)
    return pl.pallas_call(
        _mm_res_kernel,
        out_shape=jax.ShapeDtypeStruct((m, n), F32),
        grid=(m // tm, n // tn, k // tk),
        in_specs=[
            pl.BlockSpec((tm, tk), lambda i, j, kk: (i, kk)),
            pl.BlockSpec((tk, tn), lambda i, j, kk: (kk, j)),
            pl.BlockSpec((tm, tn), lambda i, j, kk: (i, j)),
        ],
        out_specs=pl.BlockSpec((tm, tn), lambda i, j, kk: (i, j)),
        compiler_params=_params("parallel", "parallel", "arbitrary"),
        name="mm_res",
    )(a, w, res)


def _prep_a_kernel(qkv_ref, qg_ref, kg_ref, cos_ref, slo_ref, shi_ref, q_ref, k_ref, v_ref):
    cos = cos_ref[...]
    slo = slo_ref[...]
    shi = shi_ref[...]

    def head(x, g):
        y = _rms_rows(x, g)
        return y * cos + pltpu.roll(y, HEAD_DIM - 32, 1) * slo + pltpu.roll(y, 32, 1) * shi

    nq = A_Q_HEADS * HEAD_DIM
    nkv = A_KV_HEADS * HEAD_DIM
    qg = qg_ref[...] * (ATTN_SCALE * LOG2E)
    kg = kg_ref[...]
    ones = jnp.ones((v_ref.shape[0], HEAD_DIM), BF16)
    for h in range(A_Q_HEADS):
        sl = slice(h * HEAD_DIM, (h + 1) * HEAD_DIM)
        q_ref[:, sl] = head(qkv_ref[:, sl], qg).astype(BF16)
    for h in range(A_KV_HEADS):
        sl = slice(h * HEAD_DIM, (h + 1) * HEAD_DIM)
        k_ref[:, sl] = head(qkv_ref[:, nq + h * HEAD_DIM:nq + (h + 1) * HEAD_DIM], kg).astype(BF16)
        vsl = slice(nq + nkv + h * HEAD_DIM, nq + nkv + (h + 1) * HEAD_DIM)
        v_ref[:, 2 * h * HEAD_DIM:(2 * h + 1) * HEAD_DIM] = qkv_ref[:, vsl].astype(BF16)
        v_ref[:, (2 * h + 1) * HEAD_DIM:(2 * h + 2) * HEAD_DIM] = ones


def prep_a(qkv, q_gain, k_gain, tabs, seq):
    m, n = qkv.shape
    nq = A_Q_HEADS * HEAD_DIM
    nkv = A_KV_HEADS * HEAD_DIM
    tm = PREP_TM
    nt = seq // tm
    tab_spec = pl.BlockSpec((tm, HEAD_DIM), lambda i: (i % nt, 0))
    g_spec = pl.BlockSpec((1, HEAD_DIM), lambda i: (0, 0))
    return pl.pallas_call(
        _prep_a_kernel,
        out_shape=(
            jax.ShapeDtypeStruct((m, nq), BF16),
            jax.ShapeDtypeStruct((m, nkv), BF16),
            jax.ShapeDtypeStruct((m, 2 * nkv), BF16),
        ),
        grid=(m // tm,),
        in_specs=[pl.BlockSpec((tm, n), lambda i: (i, 0)), g_spec, g_spec, tab_spec, tab_spec, tab_spec],
        out_specs=(
            pl.BlockSpec((tm, nq), lambda i: (i, 0)),
            pl.BlockSpec((tm, nkv), lambda i: (i, 0)),
            pl.BlockSpec((tm, 2 * nkv), lambda i: (i, 0)),
        ),
        compiler_params=_params("parallel"),
        name="prep_a",
    )(qkv, q_gain.reshape(1, HEAD_DIM), k_gain.reshape(1, HEAD_DIM), *tabs)


def _stack_q(q_ref, qs_ref, tq):
    for g in range(A_GROUP):
        qs_ref[g * tq:(g + 1) * tq, :] = q_ref[0, :, g * HEAD_DIM:(g + 1) * HEAD_DIM]


def _unstack_o(o, o_ref, tq):
    for g in range(A_GROUP):
        o_ref[0, :, g * HEAD_DIM:(g + 1) * HEAD_DIM] = o[g * tq:(g + 1) * tq, :].astype(o_ref.dtype)


def _flash_a_bounded_kernel(q_ref, k_ref, v_ref, o_ref, qs_ref, acc_ref, *, tq, tk, nk):
    _stack_q(q_ref, qs_ref, tq)
    acc_ref[...] = jnp.zeros(acc_ref.shape, F32)

    def step(c, carry):
        start = pl.multiple_of(c * tk, tk)
        kc = k_ref[0, pl.ds(start, tk), :]
        vc = v_ref[0, pl.ds(start, tk), :]
        s = lax.dot_general(qs_ref[...], kc, (((1,), (1,)), ((), ())), preferred_element_type=F32)
        acc_ref[...] += jnp.dot(jnp.exp2(s).astype(BF16), vc, preferred_element_type=F32)
        return carry

    lax.fori_loop(0, nk, step, 0, unroll=8)
    acc = acc_ref[...]
    _unstack_o(acc[:, :HEAD_DIM] / acc[:, HEAD_DIM:], o_ref, tq)


def _flash_a_online_kernel(q_ref, k_ref, v_ref, o_ref, qs_ref, m_ref, acc_ref, *, tq, tk, nk):
    _stack_q(q_ref, qs_ref, tq)
    m_ref[...] = jnp.full(m_ref.shape, NEG_BIG, F32)
    acc_ref[...] = jnp.zeros(acc_ref.shape, F32)

    def step(c, carry):
        start = pl.multiple_of(c * tk, tk)
        kc = k_ref[0, pl.ds(start, tk), :]
        vc = v_ref[0, pl.ds(start, tk), :]
        s = lax.dot_general(qs_ref[...], kc, (((1,), (1,)), ((), ())), preferred_element_type=F32)
        m_prev = m_ref[...]
        m_new = jnp.maximum(m_prev, jnp.max(s, axis=1, keepdims=True))
        alpha = jnp.exp2(m_prev - m_new)
        p = jnp.exp2(s - m_new)
        acc_ref[...] = alpha * acc_ref[...] + jnp.dot(p.astype(BF16), vc, preferred_element_type=F32)
        m_ref[...] = m_new
        return carry

    lax.fori_loop(0, nk, step, 0)
    acc = acc_ref[...]
    _unstack_o(acc[:, :HEAD_DIM] / acc[:, HEAD_DIM:], o_ref, tq)


def _flash_a_call(q, k, v, *, bounded):
    bsz, seq, nq = q.shape
    tq, tk = FLASH_TQ, FLASH_TK
    gw = A_GROUP * HEAD_DIM
    rows = A_GROUP * tq
    scratch = [pltpu.VMEM((rows, HEAD_DIM), BF16)]
    if not bounded:
        scratch.append(pltpu.VMEM((rows, 1), F32))
    scratch.append(pltpu.VMEM((rows, 2 * HEAD_DIM), F32))
    body = _flash_a_bounded_kernel if bounded else _flash_a_online_kernel
    return pl.pallas_call(
        functools.partial(body, tq=tq, tk=tk, nk=seq // tk),
        out_shape=jax.ShapeDtypeStruct((bsz, seq, nq), BF16),
        grid=(bsz, A_KV_HEADS, seq // tq),
        in_specs=[
            pl.BlockSpec((1, tq, gw), lambda b, h, i: (b, i, h)),
            pl.BlockSpec((1, seq, HEAD_DIM), lambda b, h, i: (b, 0, h)),
            pl.BlockSpec((1, seq, 2 * HEAD_DIM), lambda b, h, i: (b, 0, h)),
        ],
        out_specs=pl.BlockSpec((1, tq, gw), lambda b, h, i: (b, i, h)),
        scratch_shapes=scratch,
        compiler_params=_params("parallel", "parallel", "arbitrary"),
        name="flash_a_bounded" if bounded else "flash_a_online",
    )(q, k, v)


def flash_a(q, k, v, q_gain, k_gain):
    bound = HEAD_DIM * ATTN_SCALE * jnp.max(jnp.abs(q_gain)) * jnp.max(jnp.abs(k_gain))
    return lax.cond(
        bound <= BOUNDED_SCORE_LIMIT,
        functools.partial(_flash_a_call, bounded=True),
        functools.partial(_flash_a_call, bounded=False),
        q, k, v,
    )


def _prep_b_kernel(qkv_ref, cos_ref, sin_ref, o_ref):
    cos = cos_ref[...]
    sin = sin_ref[...]
    nh = N_B_GROUPS * B_HEADS
    for h in range(2 * nh):
        sl = slice(h * HEAD_DIM, (h + 1) * HEAD_DIM)
        x = qkv_ref[:, sl]
        if h < nh:
            x = x * ATTN_SCALE
        o_ref[:, sl] = (x * cos + pltpu.roll(x, HEAD_DIM // 2, 1) * sin).astype(BF16)
    o_ref[:, 2 * nh * HEAD_DIM:] = qkv_ref[:, 2 * nh * HEAD_DIM:].astype(BF16)


def prep_b(qkv, cos, sin_signed, seq):
    m, n = qkv.shape
    tm = PREP_TM
    nt = seq // tm
    tab_spec = pl.BlockSpec((tm, HEAD_DIM), lambda i: (i % nt, 0))
    return pl.pallas_call(
        _prep_b_kernel,
        out_shape=jax.ShapeDtypeStruct((m, n), BF16),
        grid=(m // tm,),
        in_specs=[pl.BlockSpec((tm, n), lambda i: (i, 0)), tab_spec, tab_spec],
        out_specs=pl.BlockSpec((tm, n), lambda i: (i, 0)),
        compiler_params=_params("parallel"),
        name="prep_b",
    )(qkv, cos, sin_signed)


def _band_masks():
    out = []
    qi = np.arange(BAND_Q)[:, None]
    for win, dil in B_GROUPS:
        nb = _band_halo_blocks(win, dil)
        kj = np.arange(BAND_Q * (2 * nb + 1))[None, :] - BAND_Q * nb
        rel = kj - qi
        ok = (rel % dil == 0) & (np.abs(rel) <= win // 2)
        out.append(np.where(ok, 0.0, NEG_BIG).astype(np.float32))
    return np.concatenate(out, axis=1)


def _band_halo_blocks(win, dil):
    return -(-(win // 2) // BAND_Q)


def _band_b_kernel(*refs, seq):
    q_refs = refs[0:3]
    k_refs = refs[3:12]
    v_refs = refs[12:21]
    mask_ref = refs[21]
    o_ref = refs[22]
    kcat = refs[23:26]
    vcat = refs[26:29]
    tile = BAND_TILE
    i = pl.program_id(1)

    for g in range(N_B_GROUPS):
        for part in range(3):
            kcat[g][part * tile:(part + 1) * tile, :] = k_refs[3 * g + part][0]
            vcat[g][part * tile:(part + 1) * tile, :] = v_refs[3 * g + part][0]

    pos = (i - 1) * tile + lax.broadcasted_iota(jnp.int32, (1, 3 * tile), 1)
    vrow = jnp.where((pos >= 0) & (pos < seq), 0.0, NEG_BIG).astype(F32)

    spans = []
    off = 0
    for win, dil in B_GROUPS:
        nb = _band_halo_blocks(win, dil)
        width = BAND_Q * (2 * nb + 1)
        spans.append((nb, width, off))
        off += width

    for j in range(tile // BAND_Q):
        s_list = []
        for g, (nb, width, moff) in enumerate(spans):
            lo = tile + BAND_Q * (j - nb)
            qj = q_refs[g][0, j * BAND_Q:(j + 1) * BAND_Q, :]
            kj = kcat[g][lo:lo + width, :]
            s = lax.dot_general(qj, kj, (((1,), (1,)), ((), ())), preferred_element_type=F32)
            s_list.append(s + (mask_ref[:, moff:moff + width] + vrow[:, lo:lo + width]))
        m = s_list[0].max(axis=1, keepdims=True)
        for s in s_list[1:]:
            m = jnp.maximum(m, s.max(axis=1, keepdims=True))
        l = jnp.zeros((BAND_Q, 1), F32)
        o = jnp.zeros((BAND_Q, HEAD_DIM), F32)
        for g, (nb, width, moff) in enumerate(spans):
            lo = tile + BAND_Q * (j - nb)
            p = jnp.exp(s_list[g] - m)
            l = l + p.sum(axis=1, keepdims=True)
            o = o + jnp.dot(p.astype(BF16), vcat[g][lo:lo + width, :], preferred_element_type=F32)
        o_ref[0, j * BAND_Q:(j + 1) * BAND_Q, :] = (o / l).astype(o_ref.dtype)


def band_b(qkv, masks):
    bsz, seq, _ = qkv.shape
    tile = BAND_TILE
    nt = seq // tile
    nh = N_B_GROUPS * B_HEADS

    def col(which, g):
        return lambda b, i, h: which * nh + g * B_HEADS + h

    def spec(which, g, shift):
        def imap(b, i, h):
            return (b, jnp.clip(i + shift, 0, nt - 1), which * nh + g * B_HEADS + h)
        return pl.BlockSpec((1, tile, HEAD_DIM), imap)

    in_specs = [spec(0, g, 0) for g in range(N_B_GROUPS)]
    in_specs += [spec(1, g, s) for g in range(N_B_GROUPS) for s in (-1, 0, 1)]
    in_specs += [spec(2, g, s) for g in range(N_B_GROUPS) for s in (-1, 0, 1)]
    in_specs += [pl.BlockSpec(masks.shape, lambda b, i, h: (0, 0))]
    scratch = [pltpu.VMEM((3 * tile, HEAD_DIM), BF16) for _ in range(2 * N_B_GROUPS)]
    return pl.pallas_call(
        functools.partial(_band_b_kernel, seq=seq),
        out_shape=jax.ShapeDtypeStruct((bsz, seq, B_HEADS * HEAD_DIM), BF16),
        grid=(bsz, nt, B_HEADS),
        in_specs=in_specs,
        out_specs=pl.BlockSpec((1, tile, HEAD_DIM), lambda b, i, h: (b, i, h)),
        scratch_shapes=scratch,
        compiler_params=_params("parallel", "parallel", "arbitrary"),
        name="band_b",
    )(*([qkv] * 21), masks)


def _cross_kernel(x_ref, g_ref, wq_ref, kv_ref, wo_ref, o_ref):
    x = x_ref[...]
    xn = _rms_rows(x, g_ref[...]).astype(BF16)
    q = (jnp.dot(xn, wq_ref[...], preferred_element_type=F32) * ATTN_SCALE).astype(BF16)
    nkv = C_HEADS * HEAD_DIM
    outs = []
    for h in range(C_HEADS):
        sl = slice(h * HEAD_DIM, (h + 1) * HEAD_DIM)
        kh = kv_ref[0, :, sl]
        vh = kv_ref[0, :, nkv + h * HEAD_DIM:nkv + (h + 1) * HEAD_DIM]
        s = lax.dot_general(q[:, sl], kh, (((1,), (1,)), ((), ())), preferred_element_type=F32)
        p = jnp.exp(s - s.max(axis=1, keepdims=True))
        l = p.sum(axis=1, keepdims=True)
        outs.append((jnp.dot(p.astype(BF16), vh, preferred_element_type=F32) / l).astype(BF16))
    o = jnp.concatenate(outs, axis=1)
    o_ref[...] = x + jnp.dot(o, wo_ref[...], preferred_element_type=F32)


def cross_block(x, g, wq, kv, wo, seq):
    m, d = x.shape
    tm = CROSS_TM
    nt = seq // tm
    nkv = C_HEADS * HEAD_DIM
    mem_len = kv.shape[1]
    return pl.pallas_call(
        _cross_kernel,
        out_shape=jax.ShapeDtypeStruct((m, d), F32),
        grid=(m // tm,),
        in_specs=[
            pl.BlockSpec((tm, d), lambda i: (i, 0)),
            pl.BlockSpec((1, d), lambda i: (0, 0)),
            pl.BlockSpec((d, nkv), lambda i: (0, 0)),
            pl.BlockSpec((1, mem_len, 2 * nkv), lambda i: (i // nt, 0, 0)),
            pl.BlockSpec((nkv, d), lambda i: (0, 0)),
        ],
        out_specs=pl.BlockSpec((tm, d), lambda i: (i, 0)),
        compiler_params=_params("parallel"),
        name="cross_block",
    )(x, g.reshape(1, d), wq, kv, wo)


def _norm_kernel(x_ref, g_ref, o_ref):
    o_ref[...] = _rms_rows(x_ref[...], g_ref[...])


def final_norm_rows(x, g):
    m, d = x.shape
    tm = NORM_TM
    return pl.pallas_call(
        _norm_kernel,
        out_shape=jax.ShapeDtypeStruct((m, d), F32),
        grid=(m // tm,),
        in_specs=[pl.BlockSpec((tm, d), lambda i: (i, 0)), pl.BlockSpec((1, d), lambda i: (0, 0))],
        out_specs=pl.BlockSpec((tm, d), lambda i: (i, 0)),
        compiler_params=_params("parallel"),
        name="final_norm",
    )(x, g.reshape(1, d))


def _rope_angles(pos, dim):
    inv = ROPE_THETA ** (-jnp.arange(0, dim, 2, dtype=F32) / dim)
    ang = pos[:, None] * inv[None, :]
    return jnp.concatenate([ang, ang], axis=-1)


def _rope_tables(seq):
    rows = seq // GRID_W
    row = jnp.repeat(jnp.arange(rows, dtype=F32), GRID_W)
    col = jnp.tile(jnp.arange(GRID_W, dtype=F32), rows)
    half = HEAD_DIM // 2
    ang = jnp.concatenate([_rope_angles(row, half), _rope_angles(col, half)], axis=-1)
    cos_a, sin_a = jnp.cos(ang), jnp.sin(ang)
    lane = jnp.arange(HEAD_DIM) % half
    s_lo = jnp.where(lane < half // 2, -sin_a, 0.0)
    s_hi = jnp.where(lane >= half // 2, sin_a, 0.0)
    a1 = _rope_angles(jnp.arange(seq, dtype=F32), HEAD_DIM)
    cos_b = jnp.cos(a1)
    sin_b = jnp.where(jnp.arange(HEAD_DIM) < half, -jnp.sin(a1), jnp.sin(a1))
    return (cos_a, s_lo, s_hi), (cos_b, sin_b)


def _trunk(x, mem, wts, tabs_a, tabs_b, masks):
    bsz, seq, d = x.shape
    mem_len = mem.shape[1]
    x = x.reshape(bsz * seq, d)
    mem = mem.reshape(bsz * mem_len, d)
    depth = wts["norm_mix"].shape[0]
    for i in range(depth):
        j = i // 2
        if i % 2 == 0:
            qkv = mm_norm(x, wts["norm_mix"][i], wts["a_wqkv"][j])
            q, k, v = prep_a(qkv, wts["a_q_gain"][j], wts["a_k_gain"][j], tabs_a, seq)
            o = flash_a(q.reshape(bsz, seq, -1), k.reshape(bsz, seq, -1), v.reshape(bsz, seq, -1),
                        wts["a_q_gain"][j], wts["a_k_gain"][j])
            x = mm_res(o.reshape(bsz * seq, -1), wts["a_wo"][j], x)
        else:
            qkv = mm_norm(x, wts["norm_mix"][i], wts["b_wqkv"][j])
            qkv = prep_b(qkv, tabs_b[0], tabs_b[1], seq)
            o = band_b(qkv.reshape(bsz, seq, -1), masks)
            x = mm_res(o.reshape(bsz * seq, -1), wts["b_wo"][j], x)
        kv = mm_norm(mem, wts["norm_mem"][i], wts["c_wkv"][i], out_dtype=BF16)
        x = cross_block(x, wts["norm_cross"][i], wts["c_wq"][i], kv.reshape(bsz, mem_len, -1), wts["c_wo"][i], seq)
        h = mm_norm(x, wts["norm_mlp"][i], wts["w_up"][i], relu2=True, out_dtype=BF16)
        x = mm_res(h, wts["w_down"][i], x)
    return final_norm_rows(x, wts["final_norm"]).reshape(bsz, seq, d)


def kernel(x_prompt, x_sample, mem_prompt, mem_sample, norm_mix, a_wqkv, a_q_gain, a_k_gain, a_wo, b_wqkv, b_wo, norm_cross, norm_mem, c_wq, c_wkv, c_wo, norm_mlp, w_up, w_down, final_norm):
    wts = dict(
        norm_mix=norm_mix, a_wqkv=a_wqkv.astype(BF16), a_q_gain=a_q_gain, a_k_gain=a_k_gain,
        a_wo=a_wo.astype(BF16), b_wqkv=b_wqkv.astype(BF16), b_wo=b_wo.astype(BF16),
        norm_cross=norm_cross, norm_mem=norm_mem, c_wq=c_wq.astype(BF16), c_wkv=c_wkv.astype(BF16),
        c_wo=c_wo.astype(BF16), norm_mlp=norm_mlp, w_up=w_up.astype(BF16), w_down=w_down.astype(BF16),
        final_norm=final_norm,
    )
    masks = jnp.asarray(_band_masks())
    outs = []
    for x, mem in ((x_prompt, mem_prompt), (x_sample, mem_sample)):
        tabs_a, tabs_b = _rope_tables(x.shape[1])
        outs.append(_trunk(x, mem, wts, tabs_a, tabs_b, masks))
    return tuple(outs)
```

```python
import functools

import numpy as np
import jax
import jax.numpy as jnp
from jax import lax
from jax.experimental import pallas as pl
from jax.experimental.pallas import tpu as pltpu

HEAD_DIM = 128
A_Q_HEADS = 16
A_KV_HEADS = 4
A_GROUP = A_Q_HEADS // A_KV_HEADS
B_GROUPS = ((128, 1), (512, 4), (2048, 16))
N_B_GROUPS = 3
B_HEADS = 8
C_HEADS = 4
GRID_W = 64
ROPE_THETA = 10000.0
EPS = 1e-6
ATTN_SCALE = HEAD_DIM ** -0.5
LOG2E = 1.4426950408889634
NEG_BIG = -1e30
BOUNDED_SCORE_LIMIT = 60.0

F32 = jnp.float32
BF16 = jnp.bfloat16

A_HEAD_PERM = np.concatenate([np.arange(0, 32), np.arange(64, 96), np.arange(32, 64), np.arange(96, 128)])

VMEM_LIMIT_BYTES = 56 * 1024 * 1024

MM_TM = 1024
MM_TN = 1024
MM_TK = 2048
FLASH_TQ = 256
FLASH_TK = 1024
BAND_TILE = 1024
BAND_Q = 128
CROSS_TM = 512
NORM_TM = 512


def _params(*sem):
    return pltpu.CompilerParams(dimension_semantics=sem, vmem_limit_bytes=VMEM_LIMIT_BYTES)


def _rms_rows(x, g):
    ms = jnp.mean(x * x, axis=-1, keepdims=True)
    return x * lax.rsqrt(ms + EPS) * g


def _mm_norm_kernel(x_ref, g_ref, w_ref, o_ref, xn_ref, *, relu2):
    @pl.when(pl.program_id(1) == 0)
    def _():
        xn_ref[...] = _rms_rows(x_ref[...], g_ref[...]).astype(BF16)

    acc = jnp.dot(xn_ref[...], w_ref[...], preferred_element_type=F32)
    if relu2:
        acc = jnp.square(jnp.maximum(acc, 0.0))
    o_ref[...] = acc.astype(o_ref.dtype)


def mm_norm(x, g, w, *, relu2=False, out_dtype=F32):
    m, k = x.shape
    n = w.shape[1]
    tm = min(MM_TM, m)
    tn = min(MM_TN, n)
    return pl.pallas_call(
        functools.partial(_mm_norm_kernel, relu2=relu2),
        out_shape=jax.ShapeDtypeStruct((m, n), out_dtype),
        grid=(m // tm, n // tn),
        in_specs=[
            pl.BlockSpec((tm, k), lambda i, j: (i, 0)),
            pl.BlockSpec((1, k), lambda i, j: (0, 0)),
            pl.BlockSpec((k, tn), lambda i, j: (0, j)),
        ],
        out_specs=pl.BlockSpec((tm, tn), lambda i, j: (i, j)),
        scratch_shapes=[pltpu.VMEM((tm, k), BF16)],
        compiler_params=_params("parallel", "arbitrary"),
        name="mm_norm",
    )(x, g.reshape(1, k), w)


def _mm_res_kernel(a_ref, w_ref, r_ref, o_ref):
    part = jnp.dot(a_ref[...], w_ref[...], preferred_element_type=F32)

    @pl.when(pl.program_id(2) == 0)
    def _():
        o_ref[...] = r_ref[...] + part

    @pl.when(pl.program_id(2) != 0)
    def _():
        o_ref[...] += part


def mm_res(a, w, res):
    m, k = a.shape
    n = w.shape[1]
    tm = min(MM_TM, m)
    tn = min(MM_TN, n)
    tk = min(MM_TK, k)
    return pl.pallas_call(
        _mm_res_kernel,
        out_shape=jax.ShapeDtypeStruct((m, n), F32),
        grid=(m // tm, n // tn, k // tk),
        in_specs=[
            pl.BlockSpec((tm, tk), lambda i, j, kk: (i, kk)),
            pl.BlockSpec((tk, tn), lambda i, j, kk: (kk, j)),
            pl.BlockSpec((tm, tn), lambda i, j, kk: (i, j)),
        ],
        out_specs=pl.BlockSpec((tm, tn), lambda i, j, kk: (i, j)),
        compiler_params=_params("parallel", "parallel", "arbitrary"),
        name="mm_res",
    )(a, w, res)


def _rope_half(x, cos, sin_signed):
    return x * cos + pltpu.roll(x, HEAD_DIM // 2, 1) * sin_signed


def _qkv_a_kernel(x_ref, g_ref, w_ref, qg_ref, kg_ref, cos_ref, sin_ref, q_ref, k_ref, v_ref, xn_ref):
    j = pl.program_id(1)
    n_q_tiles = A_Q_HEADS // A_KV_HEADS

    @pl.when(j == 0)
    def _():
        xn_ref[...] = _rms_rows(x_ref[...], g_ref[...]).astype(BF16)

    acc = jnp.dot(xn_ref[...], w_ref[...], preferred_element_type=F32)

    def heads(gain):
        cos = cos_ref[...]
        sin = sin_ref[...]
        out = []
        for h in range(A_KV_HEADS):
            y = _rms_rows(acc[:, h * HEAD_DIM:(h + 1) * HEAD_DIM], gain)
            out.append(_rope_half(y, cos, sin).astype(BF16))
        return jnp.concatenate(out, axis=1)

    @pl.when(j < n_q_tiles)
    def _():
        q_ref[...] = heads(qg_ref[...] * (ATTN_SCALE * LOG2E))

    @pl.when(j == n_q_tiles)
    def _():
        k_ref[...] = heads(kg_ref[...])

    @pl.when(j == n_q_tiles + 1)
    def _():
        ones = jnp.ones((v_ref.shape[0], HEAD_DIM), BF16)
        for h in range(A_KV_HEADS):
            v_ref[:, 2 * h * HEAD_DIM:(2 * h + 1) * HEAD_DIM] = acc[:, h * HEAD_DIM:(h + 1) * HEAD_DIM].astype(BF16)
            v_ref[:, (2 * h + 1) * HEAD_DIM:(2 * h + 2) * HEAD_DIM] = ones


def qkv_a(x, g, w, q_gain, k_gain, cos, sin_signed, seq):
    m, k = x.shape
    nq = A_Q_HEADS * HEAD_DIM
    nkv = A_KV_HEADS * HEAD_DIM
    tm = MM_TM
    tn = nkv
    nt = seq // tm
    n_q_tiles = nq // tn
    tab_spec = pl.BlockSpec((tm, HEAD_DIM), lambda i, j: (i % nt, 0))
    g_spec = pl.BlockSpec((1, HEAD_DIM), lambda i, j: (0, 0))
    return pl.pallas_call(
        _qkv_a_kernel,
        out_shape=(
            jax.ShapeDtypeStruct((m, nq), BF16),
            jax.ShapeDtypeStruct((m, nkv), BF16),
            jax.ShapeDtypeStruct((m, 2 * nkv), BF16),
        ),
        grid=(m // tm, w.shape[1] // tn),
        in_specs=[
            pl.BlockSpec((tm, k), lambda i, j: (i, 0)),
            pl.BlockSpec((1, k), lambda i, j: (0, 0)),
            pl.BlockSpec((k, tn), lambda i, j: (0, j)),
            g_spec,
            g_spec,
            tab_spec,
            tab_spec,
        ],
        out_specs=(
            pl.BlockSpec((tm, tn), lambda i, j: (i, jnp.minimum(j, n_q_tiles - 1))),
            pl.BlockSpec((tm, nkv), lambda i, j: (i, 0)),
            pl.BlockSpec((tm, 2 * nkv), lambda i, j: (i, 0)),
        ),
        scratch_shapes=[pltpu.VMEM((tm, k), BF16)],
        compiler_params=_params("parallel", "arbitrary"),
        name="qkv_a",
    )(x, g.reshape(1, k), w, q_gain.reshape(1, HEAD_DIM), k_gain.reshape(1, HEAD_DIM), cos, sin_signed)


def _stack_q(q_ref, qs_ref, tq):
    for g in range(A_GROUP):
        qs_ref[g * tq:(g + 1) * tq, :] = q_ref[0, :, g * HEAD_DIM:(g + 1) * HEAD_DIM]


def _unstack_o(o, o_ref, tq):
    for g in range(A_GROUP):
        o_ref[0, :, g * HEAD_DIM:(g + 1) * HEAD_DIM] = o[g * tq:(g + 1) * tq, :].astype(o_ref.dtype)


def _flash_a_bounded_kernel(q_ref, k_ref, v_ref, o_ref, qs_ref, acc_ref, *, tq, tk, nk):
    _stack_q(q_ref, qs_ref, tq)
    acc_ref[...] = jnp.zeros(acc_ref.shape, F32)

    def step(c, carry):
        start = pl.multiple_of(c * tk, tk)
        kc = k_ref[0, pl.ds(start, tk), :]
        vc = v_ref[0, pl.ds(start, tk), :]
        s = lax.dot_general(qs_ref[...], kc, (((1,), (1,)), ((), ())), preferred_element_type=F32)
        acc_ref[...] += jnp.dot(jnp.exp2(s).astype(BF16), vc, preferred_element_type=F32)
        return carry

    lax.fori_loop(0, nk, step, 0, unroll=8)
    acc = acc_ref[...]
    _unstack_o(acc[:, :HEAD_DIM] / acc[:, HEAD_DIM:], o_ref, tq)


def _flash_a_online_kernel(q_ref, k_ref, v_ref, o_ref, qs_ref, m_ref, acc_ref, *, tq, tk, nk):
    _stack_q(q_ref, qs_ref, tq)
    m_ref[...] = jnp.full(m_ref.shape, NEG_BIG, F32)
    acc_ref[...] = jnp.zeros(acc_ref.shape, F32)

    def step(c, carry):
        start = pl.multiple_of(c * tk, tk)
        kc = k_ref[0, pl.ds(start, tk), :]
        vc = v_ref[0, pl.ds(start, tk), :]
        s = lax.dot_general(qs_ref[...], kc, (((1,), (1,)), ((), ())), preferred_element_type=F32)
        m_prev = m_ref[...]
        m_new = jnp.maximum(m_prev, jnp.max(s, axis=1, keepdims=True))
        alpha = jnp.exp2(m_prev - m_new)
        p = jnp.exp2(s - m_new)
        acc_ref[...] = alpha * acc_ref[...] + jnp.dot(p.astype(BF16), vc, preferred_element_type=F32)
        m_ref[...] = m_new
        return carry

    lax.fori_loop(0, nk, step, 0)
    acc = acc_ref[...]
    _unstack_o(acc[:, :HEAD_DIM] / acc[:, HEAD_DIM:], o_ref, tq)


def _flash_a_call(q, k, v, *, bounded):
    bsz, seq, nq = q.shape
    tq, tk = FLASH_TQ, FLASH_TK
    gw = A_GROUP * HEAD_DIM
    rows = A_GROUP * tq
    scratch = [pltpu.VMEM((rows, HEAD_DIM), BF16)]
    if not bounded:
        scratch.append(pltpu.VMEM((rows, 1), F32))
    scratch.append(pltpu.VMEM((rows, 2 * HEAD_DIM), F32))
    body = _flash_a_bounded_kernel if bounded else _flash_a_online_kernel
    return pl.pallas_call(
        functools.partial(body, tq=tq, tk=tk, nk=seq // tk),
        out_shape=jax.ShapeDtypeStruct((bsz, seq, nq), BF16),
        grid=(bsz, A_KV_HEADS, seq // tq),
        in_specs=[
            pl.BlockSpec((1, tq, gw), lambda b, h, i: (b, i, h)),
            pl.BlockSpec((1, seq, HEAD_DIM), lambda b, h, i: (b, 0, h)),
            pl.BlockSpec((1, seq, 2 * HEAD_DIM), lambda b, h, i: (b, 0, h)),
        ],
        out_specs=pl.BlockSpec((1, tq, gw), lambda b, h, i: (b, i, h)),
        scratch_shapes=scratch,
        compiler_params=_params("parallel", "parallel", "arbitrary"),
        name="flash_a_bounded" if bounded else "flash_a_online",
    )(q, k, v)


def flash_a(q, k, v, q_gain, k_gain):
    bound = HEAD_DIM * ATTN_SCALE * jnp.max(jnp.abs(q_gain)) * jnp.max(jnp.abs(k_gain))
    return lax.cond(
        bound <= BOUNDED_SCORE_LIMIT,
        functools.partial(_flash_a_call, bounded=True),
        functools.partial(_flash_a_call, bounded=False),
        q, k, v,
    )


def _qkv_b_kernel(x_ref, g_ref, w_ref, cos_ref, sin_ref, o_ref, xn_ref):
    j = pl.program_id(1)

    @pl.when(j == 0)
    def _():
        xn_ref[...] = _rms_rows(x_ref[...], g_ref[...]).astype(BF16)

    acc = jnp.dot(xn_ref[...], w_ref[...], preferred_element_type=F32)

    @pl.when(j < 2 * N_B_GROUPS)
    def _():
        scale = jnp.where(j < N_B_GROUPS, ATTN_SCALE * LOG2E, 1.0).astype(F32)
        cos = cos_ref[...] * scale
        sin = sin_ref[...] * scale
        for h in range(B_HEADS):
            sl = slice(h * HEAD_DIM, (h + 1) * HEAD_DIM)
            o_ref[:, sl] = _rope_half(acc[:, sl], cos, sin).astype(BF16)

    @pl.when(j >= 2 * N_B_GROUPS)
    def _():
        o_ref[...] = acc.astype(BF16)


def qkv_b(x, g, w, cos, sin_signed, seq):
    m, k = x.shape
    n = w.shape[1]
    tm = MM_TM
    tn = B_HEADS * HEAD_DIM
    nt = seq // tm
    tab_spec = pl.BlockSpec((tm, HEAD_DIM), lambda i, j: (i % nt, 0))
    return pl.pallas_call(
        _qkv_b_kernel,
        out_shape=jax.ShapeDtypeStruct((m, n), BF16),
        grid=(m // tm, n // tn),
        in_specs=[
            pl.BlockSpec((tm, k), lambda i, j: (i, 0)),
            pl.BlockSpec((1, k), lambda i, j: (0, 0)),
            pl.BlockSpec((k, tn), lambda i, j: (0, j)),
            tab_spec,
            tab_spec,
        ],
        out_specs=pl.BlockSpec((tm, tn), lambda i, j: (i, j)),
        scratch_shapes=[pltpu.VMEM((tm, k), BF16)],
        compiler_params=_params("parallel", "arbitrary"),
        name="qkv_b",
    )(x, g.reshape(1, k), w, cos, sin_signed)


def _band_masks():
    out = []
    qi = np.arange(BAND_Q)[:, None]
    for win, dil in B_GROUPS:
        nb = _band_halo_blocks(win, dil)
        kj = np.arange(BAND_Q * (2 * nb + 1))[None, :] - BAND_Q * nb
        rel = kj - qi
        ok = (rel % dil == 0) & (np.abs(rel) <= win // 2)
        out.append(np.where(ok, 0.0, NEG_BIG).astype(np.float32))
    return np.concatenate(out, axis=1)


def _band_halo_blocks(win, dil):
    return -(-(win // 2) // BAND_Q)


def _band_b_kernel(*refs):
    q_refs = refs[0:3]
    k_refs = refs[3:12]
    v_refs = refs[12:21]
    mask_ref = refs[21]
    o_ref = refs[22]
    kcat = refs[23:26]
    vcat = refs[26:29]
    tile = BAND_TILE
    i = pl.program_id(1)

    ones = jnp.ones((3 * tile, HEAD_DIM), BF16)
    for g in range(N_B_GROUPS):
        vcat[g][:, HEAD_DIM:] = ones
        for part in range(3):
            kcat[g][part * tile:(part + 1) * tile, :] = k_refs[3 * g + part][0]
            vcat[g][part * tile:(part + 1) * tile, :HEAD_DIM] = v_refs[3 * g + part][0]

    bias_prev = jnp.where(i == 0, NEG_BIG, 0.0).astype(F32)
    bias_next = jnp.where(i == pl.num_programs(1) - 1, NEG_BIG, 0.0).astype(F32)

    spans = []
    off = 0
    for win, dil in B_GROUPS:
        nb = _band_halo_blocks(win, dil)
        width = BAND_Q * (2 * nb + 1)
        spans.append((nb, width, off))
        off += width

    for j in range(tile // BAND_Q):
        s_list = []
        for g, (nb, width, moff) in enumerate(spans):
            lo = tile + BAND_Q * (j - nb)
            qj = q_refs[g][0, j * BAND_Q:(j + 1) * BAND_Q, :]
            kj = kcat[g][lo:lo + width, :]
            s = lax.dot_general(qj, kj, (((1,), (1,)), ((), ())), preferred_element_type=F32)
            s = s + mask_ref[:, moff:moff + width]
            c_prev = min(max(tile - lo, 0), width)
            c_next = min(max(2 * tile - lo, 0), width)
            parts = []
            if c_prev > 0:
                parts.append(s[:, :c_prev] + bias_prev)
            if c_next > c_prev:
                parts.append(s[:, c_prev:c_next])
            if c_next < width:
                parts.append(s[:, c_next:] + bias_next)
            s_list.append(parts[0] if len(parts) == 1 else jnp.concatenate(parts, axis=1))
        m = s_list[0].max(axis=1, keepdims=True)
        for s in s_list[1:]:
            m = jnp.maximum(m, s.max(axis=1, keepdims=True))
        o = jnp.zeros((BAND_Q, 2 * HEAD_DIM), F32)
        for g, (nb, width, moff) in enumerate(spans):
            lo = tile + BAND_Q * (j - nb)
            p = jnp.exp2(s_list[g] - m)
            o = o + jnp.dot(p.astype(BF16), vcat[g][lo:lo + width, :], preferred_element_type=F32)
        o_ref[0, j * BAND_Q:(j + 1) * BAND_Q, :] = (o[:, :HEAD_DIM] / o[:, HEAD_DIM:]).astype(o_ref.dtype)


def band_b(qkv, masks):
    bsz, seq, _ = qkv.shape
    tile = BAND_TILE
    nt = seq // tile
    nh = N_B_GROUPS * B_HEADS

    def spec(which, g, shift):
        def imap(b, i, h):
            return (b, jnp.clip(i + shift, 0, nt - 1), which * nh + g * B_HEADS + h)
        return pl.BlockSpec((1, tile, HEAD_DIM), imap)

    in_specs = [spec(0, g, 0) for g in range(N_B_GROUPS)]
    in_specs += [spec(1, g, s) for g in range(N_B_GROUPS) for s in (-1, 0, 1)]
    in_specs += [spec(2, g, s) for g in range(N_B_GROUPS) for s in (-1, 0, 1)]
    in_specs += [pl.BlockSpec(masks.shape, lambda b, i, h: (0, 0))]
    scratch = [pltpu.VMEM((3 * tile, HEAD_DIM), BF16) for _ in range(N_B_GROUPS)]
    scratch += [pltpu.VMEM((3 * tile, 2 * HEAD_DIM), BF16) for _ in range(N_B_GROUPS)]
    return pl.pallas_call(
        _band_b_kernel,
        out_shape=jax.ShapeDtypeStruct((bsz, seq, B_HEADS * HEAD_DIM), BF16),
        grid=(bsz, nt, B_HEADS),
        in_specs=in_specs,
        out_specs=pl.BlockSpec((1, tile, HEAD_DIM), lambda b, i, h: (b, i, h)),
        scratch_shapes=scratch,
        compiler_params=_params("parallel", "parallel", "arbitrary"),
        name="band_b",
    )(*([qkv] * 21), masks)


def _cross_kernel(x_ref, g_ref, wq_ref, kv_ref, wo_ref, o_ref):
    x = x_ref[...]
    xn = _rms_rows(x, g_ref[...]).astype(BF16)
    q = (jnp.dot(xn, wq_ref[...], preferred_element_type=F32) * ATTN_SCALE).astype(BF16)
    nkv = C_HEADS * HEAD_DIM
    outs = []
    for h in range(C_HEADS):
        sl = slice(h * HEAD_DIM, (h + 1) * HEAD_DIM)
        kh = kv_ref[0, :, sl]
        vh = kv_ref[0, :, nkv + h * HEAD_DIM:nkv + (h + 1) * HEAD_DIM]
        s = lax.dot_general(q[:, sl], kh, (((1,), (1,)), ((), ())), preferred_element_type=F32)
        p = jnp.exp(s - s.max(axis=1, keepdims=True))
        l = p.sum(axis=1, keepdims=True)
        outs.append((jnp.dot(p.astype(BF16), vh, preferred_element_type=F32) / l).astype(BF16))
    o = jnp.concatenate(outs, axis=1)
    o_ref[...] = x + jnp.dot(o, wo_ref[...], preferred_element_type=F32)


def cross_block(x, g, wq, kv, wo, seq):
    m, d = x.shape
    tm = CROSS_TM
    nt = seq // tm
    nkv = C_HEADS * HEAD_DIM
    mem_len = kv.shape[1]
    return pl.pallas_call(
        _cross_kernel,
        out_shape=jax.ShapeDtypeStruct((m, d), F32),
        grid=(m // tm,),
        in_specs=[
            pl.BlockSpec((tm, d), lambda i: (i, 0)),
            pl.BlockSpec((1, d), lambda i: (0, 0)),
            pl.BlockSpec((d, nkv), lambda i: (0, 0)),
            pl.BlockSpec((1, mem_len, 2 * nkv), lambda i: (i // nt, 0, 0)),
            pl.BlockSpec((nkv, d), lambda i: (0, 0)),
        ],
        out_specs=pl.BlockSpec((tm, d), lambda i: (i, 0)),
        compiler_params=_params("parallel"),
        name="cross_block",
    )(x, g.reshape(1, d), wq, kv, wo)


def _norm_kernel(x_ref, g_ref, o_ref):
    o_ref[...] = _rms_rows(x_ref[...], g_ref[...])


def final_norm_rows(x, g):
    m, d = x.shape
    tm = NORM_TM
    return pl.pallas_call(
        _norm_kernel,
        out_shape=jax.ShapeDtypeStruct((m, d), F32),
        grid=(m // tm,),
        in_specs=[pl.BlockSpec((tm, d), lambda i: (i, 0)), pl.BlockSpec((1, d), lambda i: (0, 0))],
        out_specs=pl.BlockSpec((tm, d), lambda i: (i, 0)),
        compiler_params=_params("parallel"),
        name="final_norm",
    )(x, g.reshape(1, d))


def _rope_angles(pos, dim):
    inv = ROPE_THETA ** (-jnp.arange(0, dim, 2, dtype=F32) / dim)
    ang = pos[:, None] * inv[None, :]
    return jnp.concatenate([ang, ang], axis=-1)


def _rope_tables(seq):
    rows = seq // GRID_W
    row = jnp.repeat(jnp.arange(rows, dtype=F32), GRID_W)
    col = jnp.tile(jnp.arange(GRID_W, dtype=F32), rows)
    half = HEAD_DIM // 2
    ang = jnp.concatenate([_rope_angles(row, half), _rope_angles(col, half)], axis=-1)
    low = jnp.arange(HEAD_DIM) < half
    cos_a = jnp.cos(ang)[:, A_HEAD_PERM]
    sin_a = jnp.sin(ang)[:, A_HEAD_PERM]
    sin_a = jnp.where(low, -sin_a, sin_a)
    a1 = _rope_angles(jnp.arange(seq, dtype=F32), HEAD_DIM)
    cos_b = jnp.cos(a1)
    sin_b = jnp.where(low, -jnp.sin(a1), jnp.sin(a1))
    return (cos_a, sin_a), (cos_b, sin_b)


def _permute_a_heads(a_wqkv, a_q_gain, a_k_gain):
    n_qk = (A_Q_HEADS + A_KV_HEADS) * HEAD_DIM
    cols = np.arange(a_wqkv.shape[-1])
    cols[:n_qk] = (cols[:n_qk].reshape(-1, HEAD_DIM)[:, A_HEAD_PERM]).reshape(-1)
    return a_wqkv[..., cols], a_q_gain[..., A_HEAD_PERM], a_k_gain[..., A_HEAD_PERM]


def _trunk(x, mem, wts, tabs_a, tabs_b, masks):
    bsz, seq, d = x.shape
    mem_len = mem.shape[1]
    x = x.reshape(bsz * seq, d)
    mem = mem.reshape(bsz * mem_len, d)
    depth = wts["norm_mix"].shape[0]
    for i in range(depth):
        j = i // 2
        if i % 2 == 0:
            q, k, v = qkv_a(x, wts["norm_mix"][i], wts["a_wqkv"][j], wts["a_q_gain"][j], wts["a_k_gain"][j],
                            tabs_a[0], tabs_a[1], seq)
            o = flash_a(q.reshape(bsz, seq, -1), k.reshape(bsz, seq, -1), v.reshape(bsz, seq, -1),
                        wts["a_q_gain"][j], wts["a_k_gain"][j])
            x = mm_res(o.reshape(bsz * seq, -1), wts["a_wo"][j], x)
        else:
            qkv = qkv_b(x, wts["norm_mix"][i], wts["b_wqkv"][j], tabs_b[0], tabs_b[1], seq)
            o = band_b(qkv.reshape(bsz, seq, -1), masks)
            x = mm_res(o.reshape(bsz * seq, -1), wts["b_wo"][j], x)
        kv = mm_norm(mem, wts["norm_mem"][i], wts["c_wkv"][i], out_dtype=BF16)
        x = cross_block(x, wts["norm_cross"][i], wts["c_wq"][i], kv.reshape(bsz, mem_len, -1), wts["c_wo"][i], seq)
        h = mm_norm(x, wts["norm_mlp"][i], wts["w_up"][i], relu2=True, out_dtype=BF16)
        x = mm_res(h, wts["w_down"][i], x)
    return final_norm_rows(x, wts["final_norm"]).reshape(bsz, seq, d)


def kernel(x_prompt, x_sample, mem_prompt, mem_sample, norm_mix, a_wqkv, a_q_gain, a_k_gain, a_wo, b_wqkv, b_wo, norm_cross, norm_mem, c_wq, c_wkv, c_wo, norm_mlp, w_up, w_down, final_norm):
    a_wqkv, a_q_gain, a_k_gain = _permute_a_heads(a_wqkv, a_q_gain, a_k_gain)
    wts = dict(
        norm_mix=norm_mix, a_wqkv=a_wqkv.astype(BF16), a_q_gain=a_q_gain, a_k_gain=a_k_gain,
        a_wo=a_wo.astype(BF16), b_wqkv=b_wqkv.astype(BF16), b_wo=b_wo.astype(BF16),
        norm_cross=norm_cross, norm_mem=norm_mem, c_wq=c_wq.astype(BF16), c_wkv=c_wkv.astype(BF16),
        c_wo=c_wo.astype(BF16), norm_mlp=norm_mlp, w_up=w_up.astype(BF16), w_down=w_down.astype(BF16),
        final_norm=final_norm,
    )
    masks = jnp.asarray(_band_masks())
    outs = []
    for x, mem in ((x_prompt, mem_prompt), (x_sample, mem_sample)):
        tabs_a, tabs_b = _rope_tables(x.shape[1])
        outs.append(_trunk(x, mem, wts, tabs_a, tabs_b, masks))
    return tuple(outs)
```

```python
import functools

import numpy as np
import jax
import jax.numpy as jnp
from jax import lax
from jax.experimental import pallas as pl
from jax.experimental.pallas import tpu as pltpu

HEAD_DIM = 128
A_Q_HEADS = 16
A_KV_HEADS = 4
A_GROUP = A_Q_HEADS // A_KV_HEADS
B_GROUPS = ((128, 1), (512, 4), (2048, 16))
N_B_GROUPS = 3
B_HEADS = 8
C_HEADS = 4
GRID_W = 64
ROPE_THETA = 10000.0
EPS = 1e-6
ATTN_SCALE = HEAD_DIM ** -0.5
LOG2E = 1.4426950408889634
NEG_BIG = -1e30
BOUNDED_SCORE_LIMIT = 60.0

F32 = jnp.float32
BF16 = jnp.bfloat16

A_HEAD_PERM = np.concatenate([np.arange(0, 32), np.arange(64, 96), np.arange(32, 64), np.arange(96, 128)])

VMEM_LIMIT_BYTES = 56 * 1024 * 1024

MM_TM = 1024
MM_TN = 1024
MM_TK = 2048
FLASH_TQ = 256
FLASH_TK = 1024
BAND_TILE = 1024
BAND_Q = 128
CROSS_TM = 512
NORM_TM = 512


def _params(*sem):
    return pltpu.CompilerParams(dimension_semantics=sem, vmem_limit_bytes=VMEM_LIMIT_BYTES)


def _rms_rows(x, g):
    ms = jnp.mean(x * x, axis=-1, keepdims=True)
    return x * lax.rsqrt(ms + EPS) * g


def _mm_norm_kernel(x_ref, g_ref, w_ref, o_ref, xn_ref, *, relu2):
    @pl.when(pl.program_id(1) == 0)
    def _():
        xn_ref[...] = _rms_rows(x_ref[...], g_ref[...]).astype(BF16)

    acc = jnp.dot(xn_ref[...], w_ref[...], preferred_element_type=F32)
    if relu2:
        acc = jnp.square(jnp.maximum(acc, 0.0))
    o_ref[...] = acc.astype(o_ref.dtype)


def mm_norm(x, g, w, *, relu2=False, out_dtype=F32):
    m, k = x.shape
    n = w.shape[1]
    tm = min(MM_TM, m)
    tn = min(MM_TN, n)
    return pl.pallas_call(
        functools.partial(_mm_norm_kernel, relu2=relu2),
        out_shape=jax.ShapeDtypeStruct((m, n), out_dtype),
        grid=(m // tm, n // tn),
        in_specs=[
            pl.BlockSpec((tm, k), lambda i, j: (i, 0)),
            pl.BlockSpec((1, k), lambda i, j: (0, 0)),
            pl.BlockSpec((k, tn), lambda i, j: (0, j)),
        ],
        out_specs=pl.BlockSpec((tm, tn), lambda i, j: (i, j)),
        scratch_shapes=[pltpu.VMEM((tm, k), BF16)],
        compiler_params=_params("parallel", "arbitrary"),
        name="mm_norm",
    )(x, g.reshape(1, k), w)


def _mm_res_kernel(a_ref, w_ref, r_ref, o_ref):
    @pl.when(pl.program_id(2) == 0)
    def _():
        o_ref[...] = r_ref[...]

    o_ref[...] += jnp.dot(a_ref[...], w_ref[...], preferred_element_type=F32)


def mm_res(a, w, res):
    m, k = a.shape
    n = w.shape[1]
    tm = min(MM_TM, m)
    tn = min(MM_TN, n)
    tk = min(MM_TK, k)
    return pl.pallas_call(
        _mm_res_kernel,
        out_shape=jax.ShapeDtypeStruct((m, n), F32),
        grid=(m // tm, n // tn, k // tk),
        in_specs=[
            pl.BlockSpec((tm, tk), lambda i, j, kk: (i, kk)),
            pl.BlockSpec((tk, tn), lambda i, j, kk: (kk, j)),
            pl.BlockSpec((tm, tn), lambda i, j, kk: (i, j)),
        ],
        out_specs=pl.BlockSpec((tm, tn), lambda i, j, kk: (i, j)),
        compiler_params=_params("parallel", "parallel", "arbitrary"),
        name="mm_res",
    )(a, w, res)


def _rope_half(x, cos, sin_signed):
    return x * cos + pltpu.roll(x, HEAD_DIM // 2, 1) * sin_signed


def _qkv_a_kernel(x_ref, g_ref, w_ref, qg_ref, kg_ref, cos_ref, sin_ref, qk_ref, v_ref, xn_ref):
    j = pl.program_id(1)
    n_q_tiles = A_Q_HEADS // A_KV_HEADS

    @pl.when(j == 0)
    def _():
        xn_ref[...] = _rms_rows(x_ref[...], g_ref[...]).astype(BF16)

    acc = jnp.dot(xn_ref[...], w_ref[...], preferred_element_type=F32)

    @pl.when(j <= n_q_tiles)
    def _():
        gain = jnp.where(j < n_q_tiles, qg_ref[...] * (ATTN_SCALE * LOG2E), kg_ref[...])
        cos = cos_ref[...]
        sin = sin_ref[...]
        heads = []
        for h in range(A_KV_HEADS):
            sl = slice(h * HEAD_DIM, (h + 1) * HEAD_DIM)
            heads.append(_rope_half(_rms_rows(acc[:, sl], gain), cos, sin).astype(BF16))
        qk_ref[...] = jnp.concatenate(heads, axis=1)

    @pl.when(j == n_q_tiles + 1)
    def _():
        ones = jnp.ones((v_ref.shape[0], HEAD_DIM), BF16)
        for h in range(A_KV_HEADS):
            v_ref[:, 2 * h * HEAD_DIM:(2 * h + 1) * HEAD_DIM] = acc[:, h * HEAD_DIM:(h + 1) * HEAD_DIM].astype(BF16)
            v_ref[:, (2 * h + 1) * HEAD_DIM:(2 * h + 2) * HEAD_DIM] = ones


def qkv_a(x, g, w, q_gain, k_gain, cos, sin_signed, seq):
    m, k = x.shape
    nkv = A_KV_HEADS * HEAD_DIM
    tm = MM_TM
    tn = nkv
    nt = seq // tm
    n_qk_tiles = (A_Q_HEADS + A_KV_HEADS) * HEAD_DIM // tn
    tab_spec = pl.BlockSpec((tm, HEAD_DIM), lambda i, j: (i % nt, 0))
    g_spec = pl.BlockSpec((1, HEAD_DIM), lambda i, j: (0, 0))
    return pl.pallas_call(
        _qkv_a_kernel,
        out_shape=(
            jax.ShapeDtypeStruct((m, n_qk_tiles * tn), BF16),
            jax.ShapeDtypeStruct((m, 2 * nkv), BF16),
        ),
        grid=(m // tm, w.shape[1] // tn),
        in_specs=[
            pl.BlockSpec((tm, k), lambda i, j: (i, 0)),
            pl.BlockSpec((1, k), lambda i, j: (0, 0)),
            pl.BlockSpec((k, tn), lambda i, j: (0, j)),
            g_spec,
            g_spec,
            tab_spec,
            tab_spec,
        ],
        out_specs=(
            pl.BlockSpec((tm, tn), lambda i, j: (i, jnp.minimum(j, n_qk_tiles - 1))),
            pl.BlockSpec((tm, 2 * nkv), lambda i, j: (i, 0)),
        ),
        scratch_shapes=[pltpu.VMEM((tm, k), BF16)],
        compiler_params=_params("parallel", "arbitrary"),
        name="qkv_a",
    )(x, g.reshape(1, k), w, q_gain.reshape(1, HEAD_DIM), k_gain.reshape(1, HEAD_DIM), cos, sin_signed)


def _stack_q(q_ref, qs_ref, tq):
    for g in range(A_GROUP):
        qs_ref[g * tq:(g + 1) * tq, :] = q_ref[0, :, g * HEAD_DIM:(g + 1) * HEAD_DIM]


def _unstack_o(o, o_ref, tq):
    for g in range(A_GROUP):
        o_ref[0, :, g * HEAD_DIM:(g + 1) * HEAD_DIM] = o[g * tq:(g + 1) * tq, :].astype(o_ref.dtype)


def _flash_a_bounded_kernel(q_ref, k_ref, v_ref, o_ref, qs_ref, acc_ref, *, tq, tk, nk):
    _stack_q(q_ref, qs_ref, tq)
    acc_ref[...] = jnp.zeros(acc_ref.shape, F32)

    def step(c, carry):
        start = pl.multiple_of(c * tk, tk)
        kc = k_ref[0, pl.ds(start, tk), :]
        vc = v_ref[0, pl.ds(start, tk), :]
        s = lax.dot_general(qs_ref[...], kc, (((1,), (1,)), ((), ())), preferred_element_type=F32)
        acc_ref[...] += jnp.dot(jnp.exp2(s).astype(BF16), vc, preferred_element_type=F32)
        return carry

    lax.fori_loop(0, nk, step, 0, unroll=8)
    acc = acc_ref[...]
    _unstack_o(acc[:, :HEAD_DIM] / acc[:, HEAD_DIM:], o_ref, tq)


def _flash_a_online_kernel(q_ref, k_ref, v_ref, o_ref, qs_ref, m_ref, acc_ref, *, tq, tk, nk):
    _stack_q(q_ref, qs_ref, tq)
    m_ref[...] = jnp.full(m_ref.shape, NEG_BIG, F32)
    acc_ref[...] = jnp.zeros(acc_ref.shape, F32)

    def step(c, carry):
        start = pl.multiple_of(c * tk, tk)
        kc = k_ref[0, pl.ds(start, tk), :]
        vc = v_ref[0, pl.ds(start, tk), :]
        s = lax.dot_general(qs_ref[...], kc, (((1,), (1,)), ((), ())), preferred_element_type=F32)
        m_prev = m_ref[...]
        m_new = jnp.maximum(m_prev, jnp.max(s, axis=1, keepdims=True))
        alpha = jnp.exp2(m_prev - m_new)
        p = jnp.exp2(s - m_new)
        acc_ref[...] = alpha * acc_ref[...] + jnp.dot(p.astype(BF16), vc, preferred_element_type=F32)
        m_ref[...] = m_new
        return carry

    lax.fori_loop(0, nk, step, 0)
    acc = acc_ref[...]
    _unstack_o(acc[:, :HEAD_DIM] / acc[:, HEAD_DIM:], o_ref, tq)


def _flash_a_call(qk, v, *, bounded):
    bsz, seq, _ = qk.shape
    nq = A_Q_HEADS * HEAD_DIM
    tq, tk = FLASH_TQ, FLASH_TK
    gw = A_GROUP * HEAD_DIM
    rows = A_GROUP * tq
    scratch = [pltpu.VMEM((rows, HEAD_DIM), BF16)]
    if not bounded:
        scratch.append(pltpu.VMEM((rows, 1), F32))
    scratch.append(pltpu.VMEM((rows, 2 * HEAD_DIM), F32))
    body = _flash_a_bounded_kernel if bounded else _flash_a_online_kernel
    return pl.pallas_call(
        functools.partial(body, tq=tq, tk=tk, nk=seq // tk),
        out_shape=jax.ShapeDtypeStruct((bsz, seq, nq), BF16),
        grid=(bsz, A_KV_HEADS, seq // tq),
        in_specs=[
            pl.BlockSpec((1, tq, gw), lambda b, h, i: (b, i, h)),
            pl.BlockSpec((1, seq, HEAD_DIM), lambda b, h, i: (b, 0, A_Q_HEADS + h)),
            pl.BlockSpec((1, seq, 2 * HEAD_DIM), lambda b, h, i: (b, 0, h)),
        ],
        out_specs=pl.BlockSpec((1, tq, gw), lambda b, h, i: (b, i, h)),
        scratch_shapes=scratch,
        compiler_params=_params("parallel", "parallel", "arbitrary"),
        name="flash_a_bounded" if bounded else "flash_a_online",
    )(qk, qk, v)


def flash_a(qk, v, q_gain, k_gain):
    bound = HEAD_DIM * ATTN_SCALE * jnp.max(jnp.abs(q_gain)) * jnp.max(jnp.abs(k_gain))
    return lax.cond(
        bound <= BOUNDED_SCORE_LIMIT,
        functools.partial(_flash_a_call, bounded=True),
        functools.partial(_flash_a_call, bounded=False),
        qk, v,
    )


def _qkv_b_kernel(x_ref, g_ref, w_ref, cos_ref, sin_ref, o_ref, xn_ref):
    j = pl.program_id(1)

    @pl.when(j == 0)
    def _():
        xn_ref[...] = _rms_rows(x_ref[...], g_ref[...]).astype(BF16)

    acc = jnp.dot(xn_ref[...], w_ref[...], preferred_element_type=F32)

    rotary = j < 2 * N_B_GROUPS
    scale = jnp.where(j < N_B_GROUPS, ATTN_SCALE * LOG2E, 1.0).astype(F32)
    cos = jnp.where(rotary, cos_ref[...] * scale, 1.0)
    sin = jnp.where(rotary, sin_ref[...] * scale, 0.0)
    for h in range(B_HEADS):
        sl = slice(h * HEAD_DIM, (h + 1) * HEAD_DIM)
        o_ref[:, sl] = _rope_half(acc[:, sl], cos, sin).astype(BF16)


def qkv_b(x, g, w, cos, sin_signed, seq):
    m, k = x.shape
    n = w.shape[1]
    tm = MM_TM
    tn = B_HEADS * HEAD_DIM
    nt = seq // tm
    tab_spec = pl.BlockSpec((tm, HEAD_DIM), lambda i, j: (i % nt, 0))
    return pl.pallas_call(
        _qkv_b_kernel,
        out_shape=jax.ShapeDtypeStruct((m, n), BF16),
        grid=(m // tm, n // tn),
        in_specs=[
            pl.BlockSpec((tm, k), lambda i, j: (i, 0)),
            pl.BlockSpec((1, k), lambda i, j: (0, 0)),
            pl.BlockSpec((k, tn), lambda i, j: (0, j)),
            tab_spec,
            tab_spec,
        ],
        out_specs=pl.BlockSpec((tm, tn), lambda i, j: (i, j)),
        scratch_shapes=[pltpu.VMEM((tm, k), BF16)],
        compiler_params=_params("parallel", "arbitrary"),
        name="qkv_b",
    )(x, g.reshape(1, k), w, cos, sin_signed)


def _band_masks():
    out = []
    qi = np.arange(BAND_Q)[:, None]
    for win, dil in B_GROUPS:
        nb = _band_halo_blocks(win, dil)
        kj = np.arange(BAND_Q * (2 * nb + 1))[None, :] - BAND_Q * nb
        rel = kj - qi
        ok = (rel % dil == 0) & (np.abs(rel) <= win // 2)
        out.append(np.where(ok, 0.0, NEG_BIG).astype(np.float32))
    return np.concatenate(out, axis=1)


def _band_halo_blocks(win, dil):
    return -(-(win // 2) // BAND_Q)


def _band_b_kernel(*refs):
    q_refs = refs[0:3]
    k_refs = refs[3:12]
    v_refs = refs[12:21]
    mask_ref = refs[21]
    o_ref = refs[22]
    kcat = refs[23:26]
    vcat = refs[26:29]
    tile = BAND_TILE
    i = pl.program_id(1)

    ones = jnp.ones((3 * tile, HEAD_DIM), BF16)
    for g in range(N_B_GROUPS):
        vcat[g][:, HEAD_DIM:] = ones
        for part in range(3):
            kcat[g][part * tile:(part + 1) * tile, :] = k_refs[3 * g + part][0]
            vcat[g][part * tile:(part + 1) * tile, :HEAD_DIM] = v_refs[3 * g + part][0]

    bias_prev = jnp.where(i == 0, NEG_BIG, 0.0).astype(F32)
    bias_next = jnp.where(i == pl.num_programs(1) - 1, NEG_BIG, 0.0).astype(F32)

    spans = []
    off = 0
    for win, dil in B_GROUPS:
        nb = _band_halo_blocks(win, dil)
        width = BAND_Q * (2 * nb + 1)
        spans.append((nb, width, off))
        off += width

    for j in range(tile // BAND_Q):
        s_list = []
        for g, (nb, width, moff) in enumerate(spans):
            lo = tile + BAND_Q * (j - nb)
            qj = q_refs[g][0, j * BAND_Q:(j + 1) * BAND_Q, :]
            kj = kcat[g][lo:lo + width, :]
            s = lax.dot_general(qj, kj, (((1,), (1,)), ((), ())), preferred_element_type=F32)
            s = s + mask_ref[:, moff:moff + width]
            c_prev = min(max(tile - lo, 0), width)
            c_next = min(max(2 * tile - lo, 0), width)
            parts = []
            if c_prev > 0:
                parts.append(s[:, :c_prev] + bias_prev)
            if c_next > c_prev:
                parts.append(s[:, c_prev:c_next])
            if c_next < width:
                parts.append(s[:, c_next:] + bias_next)
            s_list.append(parts[0] if len(parts) == 1 else jnp.concatenate(parts, axis=1))
        m = s_list[0].max(axis=1, keepdims=True)
        for s in s_list[1:]:
            m = jnp.maximum(m, s.max(axis=1, keepdims=True))
        o = jnp.zeros((BAND_Q, 2 * HEAD_DIM), F32)
        for g, (nb, width, moff) in enumerate(spans):
            lo = tile + BAND_Q * (j - nb)
            p = jnp.exp2(s_list[g] - m)
            o = o + jnp.dot(p.astype(BF16), vcat[g][lo:lo + width, :], preferred_element_type=F32)
        o_ref[0, j * BAND_Q:(j + 1) * BAND_Q, :] = (o[:, :HEAD_DIM] / o[:, HEAD_DIM:]).astype(o_ref.dtype)


def band_b(qkv, masks):
    bsz, seq, _ = qkv.shape
    tile = BAND_TILE
    nt = seq // tile
    nh = N_B_GROUPS * B_HEADS

    def spec(which, g, shift):
        def imap(b, i, h):
            return (b, jnp.clip(i + shift, 0, nt - 1), which * nh + g * B_HEADS + h)
        return pl.BlockSpec((1, tile, HEAD_DIM), imap)

    in_specs = [spec(0, g, 0) for g in range(N_B_GROUPS)]
    in_specs += [spec(1, g, s) for g in range(N_B_GROUPS) for s in (-1, 0, 1)]
    in_specs += [spec(2, g, s) for g in range(N_B_GROUPS) for s in (-1, 0, 1)]
    in_specs += [pl.BlockSpec(masks.shape, lambda b, i, h: (0, 0))]
    scratch = [pltpu.VMEM((3 * tile, HEAD_DIM), BF16) for _ in range(N_B_GROUPS)]
    scratch += [pltpu.VMEM((3 * tile, 2 * HEAD_DIM), BF16) for _ in range(N_B_GROUPS)]
    return pl.pallas_call(
        _band_b_kernel,
        out_shape=jax.ShapeDtypeStruct((bsz, seq, B_HEADS * HEAD_DIM), BF16),
        grid=(bsz, nt, B_HEADS),
        in_specs=in_specs,
        out_specs=pl.BlockSpec((1, tile, HEAD_DIM), lambda b, i, h: (b, i, h)),
        scratch_shapes=scratch,
        compiler_params=_params("parallel", "parallel", "arbitrary"),
        name="band_b",
    )(*([qkv] * 21), masks)


def _cross_kernel(x_ref, g_ref, wq_ref, kv_ref, wo_ref, o_ref):
    x = x_ref[...]
    xn = _rms_rows(x, g_ref[...]).astype(BF16)
    q = (jnp.dot(xn, wq_ref[...], preferred_element_type=F32) * ATTN_SCALE).astype(BF16)
    nkv = C_HEADS * HEAD_DIM
    outs = []
    for h in range(C_HEADS):
        sl = slice(h * HEAD_DIM, (h + 1) * HEAD_DIM)
        kh = kv_ref[0, :, sl]
        vh = kv_ref[0, :, nkv + h * HEAD_DIM:nkv + (h + 1) * HEAD_DIM]
        s = lax.dot_general(q[:, sl], kh, (((1,), (1,)), ((), ())), preferred_element_type=F32)
        p = jnp.exp(s - s.max(axis=1, keepdims=True))
        l = p.sum(axis=1, keepdims=True)
        outs.append((jnp.dot(p.astype(BF16), vh, preferred_element_type=F32) / l).astype(BF16))
    o = jnp.concatenate(outs, axis=1)
    o_ref[...] = x + jnp.dot(o, wo_ref[...], preferred_element_type=F32)


def cross_block(x, g, wq, kv, wo, seq):
    m, d = x.shape
    tm = CROSS_TM
    nt = seq // tm
    nkv = C_HEADS * HEAD_DIM
    mem_len = kv.shape[1]
    return pl.pallas_call(
        _cross_kernel,
        out_shape=jax.ShapeDtypeStruct((m, d), F32),
        grid=(m // tm,),
        in_specs=[
            pl.BlockSpec((tm, d), lambda i: (i, 0)),
            pl.BlockSpec((1, d), lambda i: (0, 0)),
            pl.BlockSpec((d, nkv), lambda i: (0, 0)),
            pl.BlockSpec((1, mem_len, 2 * nkv), lambda i: (i // nt, 0, 0)),
            pl.BlockSpec((nkv, d), lambda i: (0, 0)),
        ],
        out_specs=pl.BlockSpec((tm, d), lambda i: (i, 0)),
        compiler_params=_params("parallel"),
        name="cross_block",
    )(x, g.reshape(1, d), wq, kv, wo)


def _norm_kernel(x_ref, g_ref, o_ref):
    o_ref[...] = _rms_rows(x_ref[...], g_ref[...])


def final_norm_rows(x, g):
    m, d = x.shape
    tm = NORM_TM
    return pl.pallas_call(
        _norm_kernel,
        out_shape=jax.ShapeDtypeStruct((m, d), F32),
        grid=(m // tm,),
        in_specs=[pl.BlockSpec((tm, d), lambda i: (i, 0)), pl.BlockSpec((1, d), lambda i: (0, 0))],
        out_specs=pl.BlockSpec((tm, d), lambda i: (i, 0)),
        compiler_params=_params("parallel"),
        name="final_norm",
    )(x, g.reshape(1, d))


def _rope_angles(pos, dim):
    inv = ROPE_THETA ** (-jnp.arange(0, dim, 2, dtype=F32) / dim)
    ang = pos[:, None] * inv[None, :]
    return jnp.concatenate([ang, ang], axis=-1)


def _rope_tables(seq):
    rows = seq // GRID_W
    row = jnp.repeat(jnp.arange(rows, dtype=F32), GRID_W)
    col = jnp.tile(jnp.arange(GRID_W, dtype=F32), rows)
    half = HEAD_DIM // 2
    ang = jnp.concatenate([_rope_angles(row, half), _rope_angles(col, half)], axis=-1)
    low = jnp.arange(HEAD_DIM) < half
    cos_a = jnp.cos(ang)[:, A_HEAD_PERM]
    sin_a = jnp.sin(ang)[:, A_HEAD_PERM]
    sin_a = jnp.where(low, -sin_a, sin_a)
    a1 = _rope_angles(jnp.arange(seq, dtype=F32), HEAD_DIM)
    cos_b = jnp.cos(a1)
    sin_b = jnp.where(low, -jnp.sin(a1), jnp.sin(a1))
    return (cos_a, sin_a), (cos_b, sin_b)


def _permute_a_heads(a_wqkv, a_q_gain, a_k_gain):
    n_qk = (A_Q_HEADS + A_KV_HEADS) * HEAD_DIM
    cols = np.arange(a_wqkv.shape[-1])
    cols[:n_qk] = (cols[:n_qk].reshape(-1, HEAD_DIM)[:, A_HEAD_PERM]).reshape(-1)
    return a_wqkv[..., cols], a_q_gain[..., A_HEAD_PERM], a_k_gain[..., A_HEAD_PERM]


def _trunk(x, mem, wts, tabs_a, tabs_b, masks):
    bsz, seq, d = x.shape
    mem_len = mem.shape[1]
    x = x.reshape(bsz * seq, d)
    mem = mem.reshape(bsz * mem_len, d)
    depth = wts["norm_mix"].shape[0]
    for i in range(depth):
        j = i // 2
        if i % 2 == 0:
            qk, v = qkv_a(x, wts["norm_mix"][i], wts["a_wqkv"][j], wts["a_q_gain"][j], wts["a_k_gain"][j],
                          tabs_a[0], tabs_a[1], seq)
            o = flash_a(qk.reshape(bsz, seq, -1), v.reshape(bsz, seq, -1), wts["a_q_gain"][j], wts["a_k_gain"][j])
            x = mm_res(o.reshape(bsz * seq, -1), wts["a_wo"][j], x)
        else:
            qkv = qkv_b(x, wts["norm_mix"][i], wts["b_wqkv"][j], tabs_b[0], tabs_b[1], seq)
            o = band_b(qkv.reshape(bsz, seq, -1), masks)
            x = mm_res(o.reshape(bsz * seq, -1), wts["b_wo"][j], x)
        kv = mm_norm(mem, wts["norm_mem"][i], wts["c_wkv"][i], out_dtype=BF16)
        x = cross_block(x, wts["norm_cross"][i], wts["c_wq"][i], kv.reshape(bsz, mem_len, -1), wts["c_wo"][i], seq)
        h = mm_norm(x, wts["norm_mlp"][i], wts["w_up"][i], relu2=True, out_dtype=BF16)
        x = mm_res(h, wts["w_down"][i], x)
    return final_norm_rows(x, wts["final_norm"]).reshape(bsz, seq, d)


def kernel(x_prompt, x_sample, mem_prompt, mem_sample, norm_mix, a_wqkv, a_q_gain, a_k_gain, a_wo, b_wqkv, b_wo, norm_cross, norm_mem, c_wq, c_wkv, c_wo, norm_mlp, w_up, w_down, final_norm):
    a_wqkv, a_q_gain, a_k_gain = _permute_a_heads(a_wqkv, a_q_gain, a_k_gain)
    wts = dict(
        norm_mix=norm_mix, a_wqkv=a_wqkv.astype(BF16), a_q_gain=a_q_gain, a_k_gain=a_k_gain,
        a_wo=a_wo.astype(BF16), b_wqkv=b_wqkv.astype(BF16), b_wo=b_wo.astype(BF16),
        norm_cross=norm_cross, norm_mem=norm_mem, c_wq=c_wq.astype(BF16), c_wkv=c_wkv.astype(BF16),
        c_wo=c_wo.astype(BF16), norm_mlp=norm_mlp, w_up=w_up.astype(BF16), w_down=w_down.astype(BF16),
        final_norm=final_norm,
    )
    masks = jnp.asarray(_band_masks())
    outs = []
    for x, mem in ((x_prompt, mem_prompt), (x_sample, mem_sample)):
        tabs_a, tabs_b = _rope_tables(x.shape[1])
        outs.append(_trunk(x, mem, wts, tabs_a, tabs_b, masks))
    return tuple(outs)
```

```python
import functools

import numpy as np
import jax
import jax.numpy as jnp
from jax import lax
from jax.experimental import pallas as pl
from jax.experimental.pallas import tpu as pltpu

HEAD_DIM = 128
A_Q_HEADS = 16
A_KV_HEADS = 4
A_GROUP = A_Q_HEADS // A_KV_HEADS
B_GROUPS = ((128, 1), (512, 4), (2048, 16))
N_B_GROUPS = 3
B_HEADS = 8
C_HEADS = 4
GRID_W = 64
ROPE_THETA = 10000.0
EPS = 1e-6
ATTN_SCALE = HEAD_DIM ** -0.5
LOG2E = 1.4426950408889634
NEG_BIG = -1e30
BOUNDED_SCORE_LIMIT = 60.0

F32 = jnp.float32
BF16 = jnp.bfloat16

A_HEAD_PERM = np.concatenate([np.arange(0, 32), np.arange(64, 96), np.arange(32, 64), np.arange(96, 128)])

VMEM_LIMIT_BYTES = 56 * 1024 * 1024

MM_TM = 1024
MM_TN = 1024
MM_TK = 2048
FLASH_TQ = 256
FLASH_TK = 1024
BAND_TILE = 2048
BAND_Q = 128
BAND_SIDE = 64
BAND_MIN_HALO = 256
CROSS_TM = 512
NORM_TM = 512


def _params(*sem):
    return pltpu.CompilerParams(dimension_semantics=sem, vmem_limit_bytes=VMEM_LIMIT_BYTES)


def _rms_rows(x, g):
    ms = jnp.mean(x * x, axis=-1, keepdims=True)
    return x * lax.rsqrt(ms + EPS) * g


def _mm_norm_kernel(x_ref, g_ref, w_ref, o_ref, xn_ref, *, relu2):
    @pl.when(pl.program_id(1) == 0)
    def _():
        xn_ref[...] = _rms_rows(x_ref[...], g_ref[...]).astype(BF16)

    acc = jnp.dot(xn_ref[...], w_ref[...], preferred_element_type=F32)
    if relu2:
        acc = jnp.square(jnp.maximum(acc, 0.0))
    o_ref[...] = acc.astype(o_ref.dtype)


def mm_norm(x, g, w, *, relu2=False, out_dtype=F32):
    m, k = x.shape
    n = w.shape[1]
    tm = min(MM_TM, m)
    tn = min(MM_TN, n)
    return pl.pallas_call(
        functools.partial(_mm_norm_kernel, relu2=relu2),
        out_shape=jax.ShapeDtypeStruct((m, n), out_dtype),
        grid=(m // tm, n // tn),
        in_specs=[
            pl.BlockSpec((tm, k), lambda i, j: (i, 0)),
            pl.BlockSpec((1, k), lambda i, j: (0, 0)),
            pl.BlockSpec((k, tn), lambda i, j: (0, j)),
        ],
        out_specs=pl.BlockSpec((tm, tn), lambda i, j: (i, j)),
        scratch_shapes=[pltpu.VMEM((tm, k), BF16)],
        compiler_params=_params("parallel", "arbitrary"),
        name="mm_norm",
    )(x, g.reshape(1, k), w)


def _mm_res_kernel(a_ref, w_ref, r_ref, o_ref):
    @pl.when(pl.program_id(2) == 0)
    def _():
        o_ref[...] = r_ref[...]

    o_ref[...] += jnp.dot(a_ref[...], w_ref[...], preferred_element_type=F32)


def mm_res(a, w, res):
    m, k = a.shape
    n = w.shape[1]
    tm = min(MM_TM, m)
    tn = min(MM_TN, n)
    tk = min(MM_TK, k)
    return pl.pallas_call(
        _mm_res_kernel,
        out_shape=jax.ShapeDtypeStruct((m, n), F32),
        grid=(m // tm, n // tn, k // tk),
        in_specs=[
            pl.BlockSpec((tm, tk), lambda i, j, kk: (i, kk)),
            pl.BlockSpec((tk, tn), lambda i, j, kk: (kk, j)),
            pl.BlockSpec((tm, tn), lambda i, j, kk: (i, j)),
        ],
        out_specs=pl.BlockSpec((tm, tn), lambda i, j, kk: (i, j)),
        compiler_params=_params("parallel", "parallel", "arbitrary"),
        name="mm_res",
    )(a, w, res)


def _rope_half(x, cos, sin_signed):
    return x * cos + pltpu.roll(x, HEAD_DIM // 2, 1) * sin_signed


def _qkv_a_kernel(x_ref, g_ref, w_ref, qg_ref, kg_ref, cos_ref, sin_ref, qk_ref, v_ref, xn_ref):
    j = pl.program_id(1)
    n_q_tiles = A_Q_HEADS // A_KV_HEADS

    @pl.when(j == 0)
    def _():
        xn_ref[...] = _rms_rows(x_ref[...], g_ref[...]).astype(BF16)

    acc = jnp.dot(xn_ref[...], w_ref[...], preferred_element_type=F32)

    @pl.when(j <= n_q_tiles)
    def _():
        gain = jnp.where(j < n_q_tiles, qg_ref[...] * (ATTN_SCALE * LOG2E), kg_ref[...])
        cos = cos_ref[...]
        sin = sin_ref[...]
        heads = []
        for h in range(A_KV_HEADS):
            sl = slice(h * HEAD_DIM, (h + 1) * HEAD_DIM)
            heads.append(_rope_half(_rms_rows(acc[:, sl], gain), cos, sin).astype(BF16))
        qk_ref[...] = jnp.concatenate(heads, axis=1)

    @pl.when(j == n_q_tiles + 1)
    def _():
        ones = jnp.ones((v_ref.shape[0], HEAD_DIM), BF16)
        for h in range(A_KV_HEADS):
            v_ref[:, 2 * h * HEAD_DIM:(2 * h + 1) * HEAD_DIM] = acc[:, h * HEAD_DIM:(h + 1) * HEAD_DIM].astype(BF16)
            v_ref[:, (2 * h + 1) * HEAD_DIM:(2 * h + 2) * HEAD_DIM] = ones


def qkv_a(x, g, w, q_gain, k_gain, cos, sin_signed, seq):
    m, k = x.shape
    nkv = A_KV_HEADS * HEAD_DIM
    tm = MM_TM
    tn = nkv
    nt = seq // tm
    n_qk_tiles = (A_Q_HEADS + A_KV_HEADS) * HEAD_DIM // tn
    tab_spec = pl.BlockSpec((tm, HEAD_DIM), lambda i, j: (i % nt, 0))
    g_spec = pl.BlockSpec((1, HEAD_DIM), lambda i, j: (0, 0))
    return pl.pallas_call(
        _qkv_a_kernel,
        out_shape=(
            jax.ShapeDtypeStruct((m, n_qk_tiles * tn), BF16),
            jax.ShapeDtypeStruct((m, 2 * nkv), BF16),
        ),
        grid=(m // tm, w.shape[1] // tn),
        in_specs=[
            pl.BlockSpec((tm, k), lambda i, j: (i, 0)),
            pl.BlockSpec((1, k), lambda i, j: (0, 0)),
            pl.BlockSpec((k, tn), lambda i, j: (0, j)),
            g_spec,
            g_spec,
            tab_spec,
            tab_spec,
        ],
        out_specs=(
            pl.BlockSpec((tm, tn), lambda i, j: (i, jnp.minimum(j, n_qk_tiles - 1))),
            pl.BlockSpec((tm, 2 * nkv), lambda i, j: (i, 0)),
        ),
        scratch_shapes=[pltpu.VMEM((tm, k), BF16)],
        compiler_params=_params("parallel", "arbitrary"),
        name="qkv_a",
    )(x, g.reshape(1, k), w, q_gain.reshape(1, HEAD_DIM), k_gain.reshape(1, HEAD_DIM), cos, sin_signed)


def _stack_q(q_ref, qs_ref, tq):
    for g in range(A_GROUP):
        qs_ref[g * tq:(g + 1) * tq, :] = q_ref[0, :, g * HEAD_DIM:(g + 1) * HEAD_DIM]


def _unstack_o(o, o_ref, tq):
    for g in range(A_GROUP):
        o_ref[0, :, g * HEAD_DIM:(g + 1) * HEAD_DIM] = o[g * tq:(g + 1) * tq, :].astype(o_ref.dtype)


def _flash_a_bounded_kernel(q_ref, k_ref, v_ref, o_ref, qs_ref, acc_ref, *, tq, tk, nk):
    _stack_q(q_ref, qs_ref, tq)
    acc_ref[...] = jnp.zeros(acc_ref.shape, F32)

    def step(c, carry):
        start = pl.multiple_of(c * tk, tk)
        kc = k_ref[0, pl.ds(start, tk), :]
        vc = v_ref[0, pl.ds(start, tk), :]
        s = lax.dot_general(qs_ref[...], kc, (((1,), (1,)), ((), ())), preferred_element_type=F32)
        acc_ref[...] += jnp.dot(jnp.exp2(s).astype(BF16), vc, preferred_element_type=F32)
        return carry

    lax.fori_loop(0, nk, step, 0, unroll=8)
    acc = acc_ref[...]
    _unstack_o(acc[:, :HEAD_DIM] / acc[:, HEAD_DIM:], o_ref, tq)


def _flash_a_online_kernel(q_ref, k_ref, v_ref, o_ref, qs_ref, m_ref, acc_ref, *, tq, tk, nk):
    _stack_q(q_ref, qs_ref, tq)
    m_ref[...] = jnp.full(m_ref.shape, NEG_BIG, F32)
    acc_ref[...] = jnp.zeros(acc_ref.shape, F32)

    def step(c, carry):
        start = pl.multiple_of(c * tk, tk)
        kc = k_ref[0, pl.ds(start, tk), :]
        vc = v_ref[0, pl.ds(start, tk), :]
        s = lax.dot_general(qs_ref[...], kc, (((1,), (1,)), ((), ())), preferred_element_type=F32)
        m_prev = m_ref[...]
        m_new = jnp.maximum(m_prev, jnp.max(s, axis=1, keepdims=True))
        alpha = jnp.exp2(m_prev - m_new)
        p = jnp.exp2(s - m_new)
        acc_ref[...] = alpha * acc_ref[...] + jnp.dot(p.astype(BF16), vc, preferred_element_type=F32)
        m_ref[...] = m_new
        return carry

    lax.fori_loop(0, nk, step, 0)
    acc = acc_ref[...]
    _unstack_o(acc[:, :HEAD_DIM] / acc[:, HEAD_DIM:], o_ref, tq)


def _flash_a_call(qk, v, *, bounded):
    bsz, seq, _ = qk.shape
    nq = A_Q_HEADS * HEAD_DIM
    tq, tk = FLASH_TQ, FLASH_TK
    gw = A_GROUP * HEAD_DIM
    rows = A_GROUP * tq
    scratch = [pltpu.VMEM((rows, HEAD_DIM), BF16)]
    if not bounded:
        scratch.append(pltpu.VMEM((rows, 1), F32))
    scratch.append(pltpu.VMEM((rows, 2 * HEAD_DIM), F32))
    body = _flash_a_bounded_kernel if bounded else _flash_a_online_kernel
    return pl.pallas_call(
        functools.partial(body, tq=tq, tk=tk, nk=seq // tk),
        out_shape=jax.ShapeDtypeStruct((bsz, seq, nq), BF16),
        grid=(bsz, A_KV_HEADS, seq // tq),
        in_specs=[
            pl.BlockSpec((1, tq, gw), lambda b, h, i: (b, i, h)),
            pl.BlockSpec((1, seq, HEAD_DIM), lambda b, h, i: (b, 0, A_Q_HEADS + h)),
            pl.BlockSpec((1, seq, 2 * HEAD_DIM), lambda b, h, i: (b, 0, h)),
        ],
        out_specs=pl.BlockSpec((1, tq, gw), lambda b, h, i: (b, i, h)),
        scratch_shapes=scratch,
        compiler_params=_params("parallel", "parallel", "arbitrary"),
        name="flash_a_bounded" if bounded else "flash_a_online",
    )(qk, qk, v)


def flash_a(qk, v, q_gain, k_gain):
    bound = HEAD_DIM * ATTN_SCALE * jnp.max(jnp.abs(q_gain)) * jnp.max(jnp.abs(k_gain))
    return lax.cond(
        bound <= BOUNDED_SCORE_LIMIT,
        functools.partial(_flash_a_call, bounded=True),
        functools.partial(_flash_a_call, bounded=False),
        qk, v,
    )


def _qkv_b_kernel(x_ref, g_ref, w_ref, cos_ref, sin_ref, o_ref, xn_ref):
    j = pl.program_id(1)

    @pl.when(j == 0)
    def _():
        xn_ref[...] = _rms_rows(x_ref[...], g_ref[...]).astype(BF16)

    acc = jnp.dot(xn_ref[...], w_ref[...], preferred_element_type=F32)

    rotary = j < 2 * N_B_GROUPS
    scale = jnp.where(j < N_B_GROUPS, ATTN_SCALE * LOG2E, 1.0).astype(F32)
    cos = jnp.where(rotary, cos_ref[...] * scale, 1.0)
    sin = jnp.where(rotary, sin_ref[...] * scale, 0.0)
    for h in range(B_HEADS):
        sl = slice(h * HEAD_DIM, (h + 1) * HEAD_DIM)
        o_ref[:, sl] = _rope_half(acc[:, sl], cos, sin).astype(BF16)


def qkv_b(x, g, w, cos, sin_signed, seq):
    m, k = x.shape
    n = w.shape[1]
    tm = MM_TM
    tn = B_HEADS * HEAD_DIM
    nt = seq // tm
    tab_spec = pl.BlockSpec((tm, HEAD_DIM), lambda i, j: (i % nt, 0))
    return pl.pallas_call(
        _qkv_b_kernel,
        out_shape=jax.ShapeDtypeStruct((m, n), BF16),
        grid=(m // tm, n // tn),
        in_specs=[
            pl.BlockSpec((tm, k), lambda i, j: (i, 0)),
            pl.BlockSpec((1, k), lambda i, j: (0, 0)),
            pl.BlockSpec((k, tn), lambda i, j: (0, j)),
            tab_spec,
            tab_spec,
        ],
        out_specs=pl.BlockSpec((tm, tn), lambda i, j: (i, j)),
        scratch_shapes=[pltpu.VMEM((tm, k), BF16)],
        compiler_params=_params("parallel", "arbitrary"),
        name="qkv_b",
    )(x, g.reshape(1, k), w, cos, sin_signed)


def _band_mask():
    rel = np.arange(2 * BAND_Q)[None, :] - np.arange(BAND_Q)[:, None]
    return np.where((rel >= 0) & (rel <= BAND_Q), 0.0, NEG_BIG).astype(np.float32)


def _band_halo(dil):
    return max(BAND_SIDE * dil, BAND_MIN_HALO)


def _rows(start, size, stride):
    return pl.ds(start, size) if stride == 1 else pl.ds(start, size, stride=stride)


def _band_b_kernel(*refs):
    ng = N_B_GROUPS
    q_refs = refs[0:ng]
    k_refs = refs[ng:4 * ng]
    v_refs = refs[4 * ng:7 * ng]
    mask_ref = refs[7 * ng]
    o_ref = refs[7 * ng + 1]
    scratch = refs[7 * ng + 2:]
    dec = [g for g, (_, dil) in enumerate(B_GROUPS) if dil > 1]
    nd = len(dec)
    qf, kf, vf = (dict(zip(dec, scratch[n * nd:(n + 1) * nd])) for n in range(3))
    og, lg, mg = (scratch[3 * nd + n * ng:3 * nd + (n + 1) * ng] for n in range(3))
    tile = BAND_TILE
    i = pl.program_id(1)

    def window(pieces, lo, h):
        hi = lo + 2 * BAND_Q
        parts = []
        if lo < 0:
            parts.append(pieces[0][0, h + lo:h, :])
        parts.append(pieces[1][0, max(lo, 0):min(hi, tile), :])
        if hi > tile:
            parts.append(pieces[2][0, 0:hi - tile, :])
        return parts[0] if len(parts) == 1 else jnp.concatenate(parts, axis=0)

    for g in dec:
        dil = B_GROUPS[g][1]
        h = _band_halo(dil)
        qf[g][...] = q_refs[g][0].astype(F32)
        for src, dst in ((k_refs, kf[g]), (v_refs, vf[g])):
            dst[0:h, :] = src[3 * g][0].astype(F32)
            dst[h:h + tile, :] = src[3 * g + 1][0].astype(F32)
            dst[h + tile:, :] = src[3 * g + 2][0].astype(F32)

    col = lax.broadcasted_iota(jnp.int32, (1, 2 * BAND_Q), 1)
    lo_row = jnp.where((i == 0) & (col < BAND_SIDE), NEG_BIG, 0.0).astype(F32)
    hi_row = jnp.where((i == pl.num_programs(1) - 1) & (col >= 2 * BAND_Q - BAND_SIDE), NEG_BIG, 0.0).astype(F32)
    mask = mask_ref[...]
    masks = {(False, False): mask, (True, False): mask + lo_row,
             (False, True): mask + hi_row, (True, True): mask + lo_row + hi_row}
    ones = jnp.ones((2 * BAND_Q, HEAD_DIM), BF16)

    for g, (_, dil) in enumerate(B_GROUPS):
        h = _band_halo(dil)
        n_sub = tile // (BAND_Q * dil)
        for mb in range(n_sub):
            for r in range(dil):
                q_rows = _rows(dil * BAND_Q * mb + r, BAND_Q, dil)
                if dil == 1:
                    qb = q_refs[g][0, q_rows, :]
                    kb = window(k_refs[3 * g:3 * g + 3], BAND_Q * mb - BAND_SIDE, h)
                    vb = window(v_refs[3 * g:3 * g + 3], BAND_Q * mb - BAND_SIDE, h)
                else:
                    k_rows = _rows(h + dil * (BAND_Q * mb - BAND_SIDE) + r, 2 * BAND_Q, dil)
                    qb = qf[g][q_rows, :].astype(BF16)
                    kb = kf[g][k_rows, :].astype(BF16)
                    vb = vf[g][k_rows, :].astype(BF16)
                vb = jnp.concatenate([vb, ones], axis=1)
                s = lax.dot_general(qb, kb, (((1,), (1,)), ((), ())), preferred_element_type=F32)
                s = s + masks[(mb == 0, mb == n_sub - 1)]
                m = s.max(axis=1, keepdims=True)
                o = jnp.dot(jnp.exp2(s - m).astype(BF16), vb, preferred_element_type=F32)
                og[g][q_rows, :] = o[:, :HEAD_DIM]
                lg[g][q_rows, :] = o[:, HEAD_DIM:]
                mg[g][q_rows, :] = jnp.broadcast_to(m, (BAND_Q, HEAD_DIM))

    chunk = 2 * BAND_Q
    for c in range(tile // chunk):
        rows = slice(c * chunk, (c + 1) * chunk)
        ms = [mg[g][rows, :] for g in range(ng)]
        m = functools.reduce(jnp.maximum, ms)
        ws = [jnp.exp2(mi - m) for mi in ms]
        num = sum(ws[g] * og[g][rows, :] for g in range(ng))
        den = sum(ws[g] * lg[g][rows, :] for g in range(ng))
        o_ref[0, rows, :] = (num / den).astype(o_ref.dtype)


def band_b(qkv, mask):
    bsz, seq, _ = qkv.shape
    tile = BAND_TILE
    nh = N_B_GROUPS * B_HEADS
    assert all(win == 2 * BAND_SIDE * dil for win, dil in B_GROUPS)
    halos = [_band_halo(dil) for _, dil in B_GROUPS]
    assert all(tile % h == 0 and tile % (BAND_Q * dil) == 0 for h, (_, dil) in zip(halos, B_GROUPS))

    def col(which, g, h):
        return which * nh + g * B_HEADS + h

    def tile_spec(which, g):
        return pl.BlockSpec((1, tile, HEAD_DIM), lambda b, i, h: (b, i, col(which, g, h)))

    def halo_spec(which, g, side):
        per_tile = tile // halos[g]
        last = seq // halos[g] - 1

        def imap(b, i, h):
            blk = i * per_tile - 1 if side < 0 else (i + 1) * per_tile
            return (b, jnp.clip(blk, 0, last), col(which, g, h))
        return pl.BlockSpec((1, halos[g], HEAD_DIM), imap)

    def kv_specs(which):
        return [s for g in range(N_B_GROUPS)
                for s in (halo_spec(which, g, -1), tile_spec(which, g), halo_spec(which, g, +1))]

    in_specs = [tile_spec(0, g) for g in range(N_B_GROUPS)] + kv_specs(1) + kv_specs(2)
    in_specs += [pl.BlockSpec(mask.shape, lambda b, i, h: (0, 0))]
    n_in = len(in_specs) - 1
    dec_halos = [h for h, (_, dil) in zip(halos, B_GROUPS) if dil > 1]
    scratch = [pltpu.VMEM((tile, HEAD_DIM), F32) for _ in dec_halos]
    scratch += [pltpu.VMEM((tile + 2 * h, HEAD_DIM), F32) for _ in range(2) for h in dec_halos]
    scratch += [pltpu.VMEM((tile, HEAD_DIM), F32) for _ in range(3 * N_B_GROUPS)]
    return pl.pallas_call(
        _band_b_kernel,
        out_shape=jax.ShapeDtypeStruct((bsz, seq, B_HEADS * HEAD_DIM), BF16),
        grid=(bsz, seq // tile, B_HEADS),
        in_specs=in_specs,
        out_specs=pl.BlockSpec((1, tile, HEAD_DIM), lambda b, i, h: (b, i, h)),
        scratch_shapes=scratch,
        compiler_params=_params("parallel", "parallel", "arbitrary"),
        name="band_b",
    )(*([qkv] * n_in), mask)


def _cross_kernel(x_ref, g_ref, wq_ref, kv_ref, wo_ref, o_ref):
    x = x_ref[...]
    xn = _rms_rows(x, g_ref[...]).astype(BF16)
    q = (jnp.dot(xn, wq_ref[...], preferred_element_type=F32) * ATTN_SCALE).astype(BF16)
    nkv = C_HEADS * HEAD_DIM
    outs = []
    for h in range(C_HEADS):
        sl = slice(h * HEAD_DIM, (h + 1) * HEAD_DIM)
        kh = kv_ref[0, :, sl]
        vh = kv_ref[0, :, nkv + h * HEAD_DIM:nkv + (h + 1) * HEAD_DIM]
        s = lax.dot_general(q[:, sl], kh, (((1,), (1,)), ((), ())), preferred_element_type=F32)
        p = jnp.exp(s - s.max(axis=1, keepdims=True))
        l = p.sum(axis=1, keepdims=True)
        outs.append((jnp.dot(p.astype(BF16), vh, preferred_element_type=F32) / l).astype(BF16))
    o = jnp.concatenate(outs, axis=1)
    o_ref[...] = x + jnp.dot(o, wo_ref[...], preferred_element_type=F32)


def cross_block(x, g, wq, kv, wo, seq):
    m, d = x.shape
    tm = CROSS_TM
    nt = seq // tm
    nkv = C_HEADS * HEAD_DIM
    mem_len = kv.shape[1]
    return pl.pallas_call(
        _cross_kernel,
        out_shape=jax.ShapeDtypeStruct((m, d), F32),
        grid=(m // tm,),
        in_specs=[
            pl.BlockSpec((tm, d), lambda i: (i, 0)),
            pl.BlockSpec((1, d), lambda i: (0, 0)),
            pl.BlockSpec((d, nkv), lambda i: (0, 0)),
            pl.BlockSpec((1, mem_len, 2 * nkv), lambda i: (i // nt, 0, 0)),
            pl.BlockSpec((nkv, d), lambda i: (0, 0)),
        ],
        out_specs=pl.BlockSpec((tm, d), lambda i: (i, 0)),
        compiler_params=_params("parallel"),
        name="cross_block",
    )(x, g.reshape(1, d), wq, kv, wo)


def _norm_kernel(x_ref, g_ref, o_ref):
    o_ref[...] = _rms_rows(x_ref[...], g_ref[...])


def final_norm_rows(x, g):
    m, d = x.shape
    tm = NORM_TM
    return pl.pallas_call(
        _norm_kernel,
        out_shape=jax.ShapeDtypeStruct((m, d), F32),
        grid=(m // tm,),
        in_specs=[pl.BlockSpec((tm, d), lambda i: (i, 0)), pl.BlockSpec((1, d), lambda i: (0, 0))],
        out_specs=pl.BlockSpec((tm, d), lambda i: (i, 0)),
        compiler_params=_params("parallel"),
        name="final_norm",
    )(x, g.reshape(1, d))


def _rope_angles(pos, dim):
    inv = ROPE_THETA ** (-jnp.arange(0, dim, 2, dtype=F32) / dim)
    ang = pos[:, None] * inv[None, :]
    return jnp.concatenate([ang, ang], axis=-1)


def _rope_tables(seq):
    rows = seq // GRID_W
    row = jnp.repeat(jnp.arange(rows, dtype=F32), GRID_W)
    col = jnp.tile(jnp.arange(GRID_W, dtype=F32), rows)
    half = HEAD_DIM // 2
    ang = jnp.concatenate([_rope_angles(row, half), _rope_angles(col, half)], axis=-1)
    low = jnp.arange(HEAD_DIM) < half
    cos_a = jnp.cos(ang)[:, A_HEAD_PERM]
    sin_a = jnp.sin(ang)[:, A_HEAD_PERM]
    sin_a = jnp.where(low, -sin_a, sin_a)
    a1 = _rope_angles(jnp.arange(seq, dtype=F32), HEAD_DIM)
    cos_b = jnp.cos(a1)
    sin_b = jnp.where(low, -jnp.sin(a1), jnp.sin(a1))
    return (cos_a, sin_a), (cos_b, sin_b)


def _permute_a_heads(a_wqkv, a_q_gain, a_k_gain):
    n_qk = (A_Q_HEADS + A_KV_HEADS) * HEAD_DIM
    cols = np.arange(a_wqkv.shape[-1])
    cols[:n_qk] = (cols[:n_qk].reshape(-1, HEAD_DIM)[:, A_HEAD_PERM]).reshape(-1)
    return a_wqkv[..., cols], a_q_gain[..., A_HEAD_PERM], a_k_gain[..., A_HEAD_PERM]


def _trunk(x, mem, wts, tabs_a, tabs_b, masks):
    bsz, seq, d = x.shape
    mem_len = mem.shape[1]
    x = x.reshape(bsz * seq, d)
    mem = mem.reshape(bsz * mem_len, d)
    depth = wts["norm_mix"].shape[0]
    for i in range(depth):
        j = i // 2
        if i % 2 == 0:
            qk, v = qkv_a(x, wts["norm_mix"][i], wts["a_wqkv"][j], wts["a_q_gain"][j], wts["a_k_gain"][j],
                          tabs_a[0], tabs_a[1], seq)
            o = flash_a(qk.reshape(bsz, seq, -1), v.reshape(bsz, seq, -1), wts["a_q_gain"][j], wts["a_k_gain"][j])
            x = mm_res(o.reshape(bsz * seq, -1), wts["a_wo"][j], x)
        else:
            qkv = qkv_b(x, wts["norm_mix"][i], wts["b_wqkv"][j], tabs_b[0], tabs_b[1], seq)
            o = band_b(qkv.reshape(bsz, seq, -1), masks)
            x = mm_res(o.reshape(bsz * seq, -1), wts["b_wo"][j], x)
        kv = mm_norm(mem, wts["norm_mem"][i], wts["c_wkv"][i], out_dtype=BF16)
        x = cross_block(x, wts["norm_cross"][i], wts["c_wq"][i], kv.reshape(bsz, mem_len, -1), wts["c_wo"][i], seq)
        h = mm_norm(x, wts["norm_mlp"][i], wts["w_up"][i], relu2=True, out_dtype=BF16)
        x = mm_res(h, wts["w_down"][i], x)
    return final_norm_rows(x, wts["final_norm"]).reshape(bsz, seq, d)


def kernel(x_prompt, x_sample, mem_prompt, mem_sample, norm_mix, a_wqkv, a_q_gain, a_k_gain, a_wo, b_wqkv, b_wo, norm_cross, norm_mem, c_wq, c_wkv, c_wo, norm_mlp, w_up, w_down, final_norm):
    a_wqkv, a_q_gain, a_k_gain = _permute_a_heads(a_wqkv, a_q_gain, a_k_gain)
    wts = dict(
        norm_mix=norm_mix, a_wqkv=a_wqkv.astype(BF16), a_q_gain=a_q_gain, a_k_gain=a_k_gain,
        a_wo=a_wo.astype(BF16), b_wqkv=b_wqkv.astype(BF16), b_wo=b_wo.astype(BF16),
        norm_cross=norm_cross, norm_mem=norm_mem, c_wq=c_wq.astype(BF16), c_wkv=c_wkv.astype(BF16),
        c_wo=c_wo.astype(BF16), norm_mlp=norm_mlp, w_up=w_up.astype(BF16), w_down=w_down.astype(BF16),
        final_norm=final_norm,
    )
    masks = jnp.asarray(_band_mask())
    outs = []
    for x, mem in ((x_prompt, mem_prompt), (x_sample, mem_sample)):
        tabs_a, tabs_b = _rope_tables(x.shape[1])
        outs.append(_trunk(x, mem, wts, tabs_a, tabs_b, masks))
    return tuple(outs)
```

```python
import functools

import numpy as np
import jax
import jax.numpy as jnp
from jax import lax
from jax.experimental import pallas as pl
from jax.experimental.pallas import tpu as pltpu

HEAD_DIM = 128
A_Q_HEADS = 16
A_KV_HEADS = 4
A_GROUP = A_Q_HEADS // A_KV_HEADS
B_GROUPS = ((128, 1), (512, 4), (2048, 16))
N_B_GROUPS = 3
B_HEADS = 8
C_HEADS = 4
GRID_W = 64
ROPE_THETA = 10000.0
EPS = 1e-6
ATTN_SCALE = HEAD_DIM ** -0.5
LOG2E = 1.4426950408889634
NEG_BIG = -1e30
BOUNDED_SCORE_LIMIT = 60.0

F32 = jnp.float32
BF16 = jnp.bfloat16

A_HEAD_PERM = np.concatenate([np.arange(0, 32), np.arange(64, 96), np.arange(32, 64), np.arange(96, 128)])

VMEM_LIMIT_BYTES = 56 * 1024 * 1024

MM_TM = 1024
MM_TN = 1024
MM_TK = 2048
FLASH_TQ = 256
FLASH_TK = 1024
BAND_TILE = 2048
BAND_Q = 128
BAND_SIDE = 64
BAND_MIN_HALO = 256
CROSS_TM = 512
NORM_TM = 512


def _params(*sem):
    return pltpu.CompilerParams(dimension_semantics=sem, vmem_limit_bytes=VMEM_LIMIT_BYTES)


def _rms_rows(x, g):
    ms = jnp.mean(x * x, axis=-1, keepdims=True)
    return x * lax.rsqrt(ms + EPS) * g


def _mm_norm_kernel(x_ref, g_ref, w_ref, o_ref, xn_ref, *, relu2):
    @pl.when(pl.program_id(1) == 0)
    def _():
        xn_ref[...] = _rms_rows(x_ref[...], g_ref[...]).astype(BF16)

    acc = jnp.dot(xn_ref[...], w_ref[...], preferred_element_type=F32)
    if relu2:
        acc = jnp.square(jnp.maximum(acc, 0.0))
    o_ref[...] = acc.astype(o_ref.dtype)


def mm_norm(x, g, w, *, relu2=False, out_dtype=F32):
    m, k = x.shape
    n = w.shape[1]
    tm = min(MM_TM, m)
    tn = min(MM_TN, n)
    return pl.pallas_call(
        functools.partial(_mm_norm_kernel, relu2=relu2),
        out_shape=jax.ShapeDtypeStruct((m, n), out_dtype),
        grid=(m // tm, n // tn),
        in_specs=[
            pl.BlockSpec((tm, k), lambda i, j: (i, 0)),
            pl.BlockSpec((1, k), lambda i, j: (0, 0)),
            pl.BlockSpec((k, tn), lambda i, j: (0, j)),
        ],
        out_specs=pl.BlockSpec((tm, tn), lambda i, j: (i, j)),
        scratch_shapes=[pltpu.VMEM((tm, k), BF16)],
        compiler_params=_params("parallel", "arbitrary"),
        name="mm_norm",
    )(x, g.reshape(1, k), w)


def _mm_res_kernel(a_ref, w_ref, r_ref, o_ref):
    @pl.when(pl.program_id(2) == 0)
    def _():
        o_ref[...] = r_ref[...]

    o_ref[...] += jnp.dot(a_ref[...], w_ref[...], preferred_element_type=F32)


def mm_res(a, w, res):
    m, k = a.shape
    n = w.shape[1]
    tm = min(MM_TM, m)
    tn = min(MM_TN, n)
    tk = min(MM_TK, k)
    return pl.pallas_call(
        _mm_res_kernel,
        out_shape=jax.ShapeDtypeStruct((m, n), F32),
        grid=(m // tm, n // tn, k // tk),
        in_specs=[
            pl.BlockSpec((tm, tk), lambda i, j, kk: (i, kk)),
            pl.BlockSpec((tk, tn), lambda i, j, kk: (kk, j)),
            pl.BlockSpec((tm, tn), lambda i, j, kk: (i, j)),
        ],
        out_specs=pl.BlockSpec((tm, tn), lambda i, j, kk: (i, j)),
        compiler_params=_params("parallel", "parallel", "arbitrary"),
        name="mm_res",
    )(a, w, res)


def _rope_half(x, cos, sin_signed):
    return x * cos + pltpu.roll(x, HEAD_DIM // 2, 1) * sin_signed


def _qkv_a_kernel(x_ref, g_ref, w_ref, qg_ref, kg_ref, cos_ref, sin_ref, qk_ref, v_ref, xn_ref):
    j = pl.program_id(1)
    n_q_tiles = A_Q_HEADS // A_KV_HEADS

    @pl.when(j == 0)
    def _():
        xn_ref[...] = _rms_rows(x_ref[...], g_ref[...]).astype(BF16)

    acc = jnp.dot(xn_ref[...], w_ref[...], preferred_element_type=F32)

    @pl.when(j <= n_q_tiles)
    def _():
        gain = jnp.where(j < n_q_tiles, qg_ref[...] * (ATTN_SCALE * LOG2E), kg_ref[...])
        cos = cos_ref[...]
        sin = sin_ref[...]
        heads = []
        for h in range(A_KV_HEADS):
            sl = slice(h * HEAD_DIM, (h + 1) * HEAD_DIM)
            heads.append(_rope_half(_rms_rows(acc[:, sl], gain), cos, sin).astype(BF16))
        qk_ref[...] = jnp.concatenate(heads, axis=1)

    @pl.when(j == n_q_tiles + 1)
    def _():
        ones = jnp.ones((v_ref.shape[0], HEAD_DIM), BF16)
        for h in range(A_KV_HEADS):
            v_ref[:, 2 * h * HEAD_DIM:(2 * h + 1) * HEAD_DIM] = acc[:, h * HEAD_DIM:(h + 1) * HEAD_DIM].astype(BF16)
            v_ref[:, (2 * h + 1) * HEAD_DIM:(2 * h + 2) * HEAD_DIM] = ones


def qkv_a(x, g, w, q_gain, k_gain, cos, sin_signed, seq):
    m, k = x.shape
    nkv = A_KV_HEADS * HEAD_DIM
    tm = MM_TM
    tn = nkv
    nt = seq // tm
    n_qk_tiles = (A_Q_HEADS + A_KV_HEADS) * HEAD_DIM // tn
    tab_spec = pl.BlockSpec((tm, HEAD_DIM), lambda i, j: (i % nt, 0))
    g_spec = pl.BlockSpec((1, HEAD_DIM), lambda i, j: (0, 0))
    return pl.pallas_call(
        _qkv_a_kernel,
        out_shape=(
            jax.ShapeDtypeStruct((m, n_qk_tiles * tn), BF16),
            jax.ShapeDtypeStruct((m, 2 * nkv), BF16),
        ),
        grid=(m // tm, w.shape[1] // tn),
        in_specs=[
            pl.BlockSpec((tm, k), lambda i, j: (i, 0)),
            pl.BlockSpec((1, k), lambda i, j: (0, 0)),
            pl.BlockSpec((k, tn), lambda i, j: (0, j)),
            g_spec,
            g_spec,
            tab_spec,
            tab_spec,
        ],
        out_specs=(
            pl.BlockSpec((tm, tn), lambda i, j: (i, jnp.minimum(j, n_qk_tiles - 1))),
            pl.BlockSpec((tm, 2 * nkv), lambda i, j: (i, 0)),
        ),
        scratch_shapes=[pltpu.VMEM((tm, k), BF16)],
        compiler_params=_params("parallel", "arbitrary"),
        name="qkv_a",
    )(x, g.reshape(1, k), w, q_gain.reshape(1, HEAD_DIM), k_gain.reshape(1, HEAD_DIM), cos, sin_signed)


def _stack_q(q_ref, qs_ref, tq):
    for g in range(A_GROUP):
        qs_ref[g * tq:(g + 1) * tq, :] = q_ref[0, :, g * HEAD_DIM:(g + 1) * HEAD_DIM]


def _unstack_o(o, o_ref, tq):
    for g in range(A_GROUP):
        o_ref[0, :, g * HEAD_DIM:(g + 1) * HEAD_DIM] = o[g * tq:(g + 1) * tq, :].astype(o_ref.dtype)


def _flash_a_bounded_kernel(q_ref, k_ref, v_ref, o_ref, qs_ref, acc_ref, *, tq, tk, nk):
    _stack_q(q_ref, qs_ref, tq)
    acc_ref[...] = jnp.zeros(acc_ref.shape, F32)

    def step(c, carry):
        start = pl.multiple_of(c * tk, tk)
        kc = k_ref[0, pl.ds(start, tk), :]
        vc = v_ref[0, pl.ds(start, tk), :]
        s = lax.dot_general(qs_ref[...], kc, (((1,), (1,)), ((), ())), preferred_element_type=F32)
        acc_ref[...] += jnp.dot(jnp.exp2(s).astype(BF16), vc, preferred_element_type=F32)
        return carry

    lax.fori_loop(0, nk, step, 0, unroll=True)
    acc = acc_ref[...]
    _unstack_o(acc[:, :HEAD_DIM] / acc[:, HEAD_DIM:], o_ref, tq)


def _flash_a_online_kernel(q_ref, k_ref, v_ref, o_ref, qs_ref, m_ref, acc_ref, *, tq, tk, nk):
    _stack_q(q_ref, qs_ref, tq)
    m_ref[...] = jnp.full(m_ref.shape, NEG_BIG, F32)
    acc_ref[...] = jnp.zeros(acc_ref.shape, F32)

    def step(c, carry):
        start = pl.multiple_of(c * tk, tk)
        kc = k_ref[0, pl.ds(start, tk), :]
        vc = v_ref[0, pl.ds(start, tk), :]
        s = lax.dot_general(qs_ref[...], kc, (((1,), (1,)), ((), ())), preferred_element_type=F32)
        m_prev = m_ref[...]
        m_new = jnp.maximum(m_prev, jnp.max(s, axis=1, keepdims=True))
        alpha = jnp.exp2(m_prev - m_new)
        p = jnp.exp2(s - m_new)
        acc_ref[...] = alpha * acc_ref[...] + jnp.dot(p.astype(BF16), vc, preferred_element_type=F32)
        m_ref[...] = m_new
        return carry

    lax.fori_loop(0, nk, step, 0)
    acc = acc_ref[...]
    _unstack_o(acc[:, :HEAD_DIM] / acc[:, HEAD_DIM:], o_ref, tq)


def _flash_a_call(qk, v, *, bounded):
    bsz, seq, _ = qk.shape
    nq = A_Q_HEADS * HEAD_DIM
    tq, tk = FLASH_TQ, FLASH_TK
    gw = A_GROUP * HEAD_DIM
    rows = A_GROUP * tq
    scratch = [pltpu.VMEM((rows, HEAD_DIM), BF16)]
    if not bounded:
        scratch.append(pltpu.VMEM((rows, 1), F32))
    scratch.append(pltpu.VMEM((rows, 2 * HEAD_DIM), F32))
    body = _flash_a_bounded_kernel if bounded else _flash_a_online_kernel
    return pl.pallas_call(
        functools.partial(body, tq=tq, tk=tk, nk=seq // tk),
        out_shape=jax.ShapeDtypeStruct((bsz, seq, nq), BF16),
        grid=(bsz, A_KV_HEADS, seq // tq),
        in_specs=[
            pl.BlockSpec((1, tq, gw), lambda b, h, i: (b, i, h)),
            pl.BlockSpec((1, seq, HEAD_DIM), lambda b, h, i: (b, 0, A_Q_HEADS + h)),
            pl.BlockSpec((1, seq, 2 * HEAD_DIM), lambda b, h, i: (b, 0, h)),
        ],
        out_specs=pl.BlockSpec((1, tq, gw), lambda b, h, i: (b, i, h)),
        scratch_shapes=scratch,
        compiler_params=_params("parallel", "parallel", "arbitrary"),
        name="flash_a_bounded" if bounded else "flash_a_online",
    )(qk, qk, v)


def flash_a(qk, v, q_gain, k_gain):
    bound = HEAD_DIM * ATTN_SCALE * jnp.max(jnp.abs(q_gain)) * jnp.max(jnp.abs(k_gain))
    return lax.cond(
        bound <= BOUNDED_SCORE_LIMIT,
        functools.partial(_flash_a_call, bounded=True),
        functools.partial(_flash_a_call, bounded=False),
        qk, v,
    )


def _qkv_b_kernel(x_ref, g_ref, w_ref, cos_ref, sin_ref, o_ref, xn_ref):
    j = pl.program_id(1)

    @pl.when(j == 0)
    def _():
        xn_ref[...] = _rms_rows(x_ref[...], g_ref[...]).astype(BF16)

    acc = jnp.dot(xn_ref[...], w_ref[...], preferred_element_type=F32)

    rotary = j < 2 * N_B_GROUPS
    scale = jnp.where(j < N_B_GROUPS, ATTN_SCALE * LOG2E, 1.0).astype(F32)
    cos = jnp.where(rotary, cos_ref[...] * scale, 1.0)
    sin = jnp.where(rotary, sin_ref[...] * scale, 0.0)
    for h in range(B_HEADS):
        sl = slice(h * HEAD_DIM, (h + 1) * HEAD_DIM)
        o_ref[:, sl] = _rope_half(acc[:, sl], cos, sin).astype(BF16)


def qkv_b(x, g, w, cos, sin_signed, seq):
    m, k = x.shape
    n = w.shape[1]
    tm = MM_TM
    tn = B_HEADS * HEAD_DIM
    nt = seq // tm
    tab_spec = pl.BlockSpec((tm, HEAD_DIM), lambda i, j: (i % nt, 0))
    return pl.pallas_call(
        _qkv_b_kernel,
        out_shape=jax.ShapeDtypeStruct((m, n), BF16),
        grid=(m // tm, n // tn),
        in_specs=[
            pl.BlockSpec((tm, k), lambda i, j: (i, 0)),
            pl.BlockSpec((1, k), lambda i, j: (0, 0)),
            pl.BlockSpec((k, tn), lambda i, j: (0, j)),
            tab_spec,
            tab_spec,
        ],
        out_specs=pl.BlockSpec((tm, tn), lambda i, j: (i, j)),
        scratch_shapes=[pltpu.VMEM((tm, k), BF16)],
        compiler_params=_params("parallel", "arbitrary"),
        name="qkv_b",
    )(x, g.reshape(1, k), w, cos, sin_signed)


def _band_mask():
    rel = np.arange(2 * BAND_Q)[None, :] - np.arange(BAND_Q)[:, None]
    return np.where((rel >= 0) & (rel <= BAND_Q), 0.0, NEG_BIG).astype(np.float32)


def _band_halo(dil):
    return max(BAND_SIDE * dil, BAND_MIN_HALO)


def _rows(start, size, stride):
    return pl.ds(start, size) if stride == 1 else pl.ds(start, size, stride=stride)


def _band_b_kernel(*refs):
    ng = N_B_GROUPS
    q_refs = refs[0:ng]
    k_refs = refs[ng:4 * ng]
    v_refs = refs[4 * ng:7 * ng]
    mask_ref = refs[7 * ng]
    o_ref = refs[7 * ng + 1]
    scratch = refs[7 * ng + 2:]
    dec = [g for g, (_, dil) in enumerate(B_GROUPS) if dil > 1]
    nd = len(dec)
    qf, kf, vf = (dict(zip(dec, scratch[n * nd:(n + 1) * nd])) for n in range(3))
    og, lg, mg = (scratch[3 * nd + n * ng:3 * nd + (n + 1) * ng] for n in range(3))
    tile = BAND_TILE
    i = pl.program_id(1)

    def window(pieces, lo, h):
        hi = lo + 2 * BAND_Q
        parts = []
        if lo < 0:
            parts.append(pieces[0][0, h + lo:h, :])
        parts.append(pieces[1][0, max(lo, 0):min(hi, tile), :])
        if hi > tile:
            parts.append(pieces[2][0, 0:hi - tile, :])
        return parts[0] if len(parts) == 1 else jnp.concatenate(parts, axis=0)

    for g in dec:
        dil = B_GROUPS[g][1]
        h = _band_halo(dil)
        qf[g][...] = q_refs[g][0].astype(F32)
        for src, dst in ((k_refs, kf[g]), (v_refs, vf[g])):
            dst[0:h, :] = src[3 * g][0].astype(F32)
            dst[h:h + tile, :] = src[3 * g + 1][0].astype(F32)
            dst[h + tile:, :] = src[3 * g + 2][0].astype(F32)

    col = lax.broadcasted_iota(jnp.int32, (1, 2 * BAND_Q), 1)
    lo_row = jnp.where((i == 0) & (col < BAND_SIDE), NEG_BIG, 0.0).astype(F32)
    hi_row = jnp.where((i == pl.num_programs(1) - 1) & (col >= 2 * BAND_Q - BAND_SIDE), NEG_BIG, 0.0).astype(F32)
    mask = mask_ref[...]
    masks = {(False, False): mask, (True, False): mask + lo_row,
             (False, True): mask + hi_row, (True, True): mask + lo_row + hi_row}
    ones = jnp.ones((2 * BAND_Q, HEAD_DIM), BF16)

    for g, (_, dil) in enumerate(B_GROUPS):
        h = _band_halo(dil)
        n_sub = tile // (BAND_Q * dil)
        for mb in range(n_sub):
            for r in range(dil):
                q_rows = _rows(dil * BAND_Q * mb + r, BAND_Q, dil)
                if dil == 1:
                    qb = q_refs[g][0, q_rows, :]
                    kb = window(k_refs[3 * g:3 * g + 3], BAND_Q * mb - BAND_SIDE, h)
                    vb = window(v_refs[3 * g:3 * g + 3], BAND_Q * mb - BAND_SIDE, h)
                else:
                    k_rows = _rows(h + dil * (BAND_Q * mb - BAND_SIDE) + r, 2 * BAND_Q, dil)
                    qb = qf[g][q_rows, :].astype(BF16)
                    kb = kf[g][k_rows, :].astype(BF16)
                    vb = vf[g][k_rows, :].astype(BF16)
                vb = jnp.concatenate([vb, ones], axis=1)
                s = lax.dot_general(qb, kb, (((1,), (1,)), ((), ())), preferred_element_type=F32)
                s = s + masks[(mb == 0, mb == n_sub - 1)]
                m = s.max(axis=1, keepdims=True)
                o = jnp.dot(jnp.exp2(s - m).astype(BF16), vb, preferred_element_type=F32)
                og[g][q_rows, :] = o[:, :HEAD_DIM]
                lg[g][q_rows, :] = o[:, HEAD_DIM:]
                mg[g][q_rows, :] = jnp.broadcast_to(m, (BAND_Q, HEAD_DIM))

    chunk = 2 * BAND_Q
    for c in range(tile // chunk):
        rows = slice(c * chunk, (c + 1) * chunk)
        ms = [mg[g][rows, :] for g in range(ng)]
        m = functools.reduce(jnp.maximum, ms)
        ws = [jnp.exp2(mi - m) for mi in ms]
        num = sum(ws[g] * og[g][rows, :] for g in range(ng))
        den = sum(ws[g] * lg[g][rows, :] for g in range(ng))
        o_ref[0, rows, :] = (num / den).astype(o_ref.dtype)


def band_b(qkv, mask):
    bsz, seq, _ = qkv.shape
    tile = BAND_TILE
    nh = N_B_GROUPS * B_HEADS
    assert all(win == 2 * BAND_SIDE * dil for win, dil in B_GROUPS)
    halos = [_band_halo(dil) for _, dil in B_GROUPS]
    assert all(tile % h == 0 and tile % (BAND_Q * dil) == 0 for h, (_, dil) in zip(halos, B_GROUPS))

    def col(which, g, h):
        return which * nh + g * B_HEADS + h

    def tile_spec(which, g):
        return pl.BlockSpec((1, tile, HEAD_DIM), lambda b, i, h: (b, i, col(which, g, h)))

    def halo_spec(which, g, side):
        per_tile = tile // halos[g]
        last = seq // halos[g] - 1

        def imap(b, i, h):
            blk = i * per_tile - 1 if side < 0 else (i + 1) * per_tile
            return (b, jnp.clip(blk, 0, last), col(which, g, h))
        return pl.BlockSpec((1, halos[g], HEAD_DIM), imap)

    def kv_specs(which):
        return [s for g in range(N_B_GROUPS)
                for s in (halo_spec(which, g, -1), tile_spec(which, g), halo_spec(which, g, +1))]

    in_specs = [tile_spec(0, g) for g in range(N_B_GROUPS)] + kv_specs(1) + kv_specs(2)
    in_specs += [pl.BlockSpec(mask.shape, lambda b, i, h: (0, 0))]
    n_in = len(in_specs) - 1
    dec_halos = [h for h, (_, dil) in zip(halos, B_GROUPS) if dil > 1]
    scratch = [pltpu.VMEM((tile, HEAD_DIM), F32) for _ in dec_halos]
    scratch += [pltpu.VMEM((tile + 2 * h, HEAD_DIM), F32) for _ in range(2) for h in dec_halos]
    scratch += [pltpu.VMEM((tile, HEAD_DIM), F32) for _ in range(3 * N_B_GROUPS)]
    return pl.pallas_call(
        _band_b_kernel,
        out_shape=jax.ShapeDtypeStruct((bsz, seq, B_HEADS * HEAD_DIM), BF16),
        grid=(bsz, seq // tile, B_HEADS),
        in_specs=in_specs,
        out_specs=pl.BlockSpec((1, tile, HEAD_DIM), lambda b, i, h: (b, i, h)),
        scratch_shapes=scratch,
        compiler_params=_params("parallel", "parallel", "arbitrary"),
        name="band_b",
    )(*([qkv] * n_in), mask)


def _cross_kernel(x_ref, g_ref, wq_ref, kv_ref, wo_ref, o_ref):
    x = x_ref[...]
    xn = _rms_rows(x, g_ref[...]).astype(BF16)
    q = (jnp.dot(xn, wq_ref[...], preferred_element_type=F32) * ATTN_SCALE).astype(BF16)
    nkv = C_HEADS * HEAD_DIM
    outs = []
    for h in range(C_HEADS):
        sl = slice(h * HEAD_DIM, (h + 1) * HEAD_DIM)
        kh = kv_ref[0, :, sl]
        vh = kv_ref[0, :, nkv + h * HEAD_DIM:nkv + (h + 1) * HEAD_DIM]
        s = lax.dot_general(q[:, sl], kh, (((1,), (1,)), ((), ())), preferred_element_type=F32)
        p = jnp.exp(s - s.max(axis=1, keepdims=True))
        l = p.sum(axis=1, keepdims=True)
        outs.append((jnp.dot(p.astype(BF16), vh, preferred_element_type=F32) / l).astype(BF16))
    o = jnp.concatenate(outs, axis=1)
    o_ref[...] = x + jnp.dot(o, wo_ref[...], preferred_element_type=F32)


def cross_block(x, g, wq, kv, wo, seq):
    m, d = x.shape
    tm = CROSS_TM
    nt = seq // tm
    nkv = C_HEADS * HEAD_DIM
    mem_len = kv.shape[1]
    return pl.pallas_call(
        _cross_kernel,
        out_shape=jax.ShapeDtypeStruct((m, d), F32),
        grid=(m // tm,),
        in_specs=[
            pl.BlockSpec((tm, d), lambda i: (i, 0)),
            pl.BlockSpec((1, d), lambda i: (0, 0)),
            pl.BlockSpec((d, nkv), lambda i: (0, 0)),
            pl.BlockSpec((1, mem_len, 2 * nkv), lambda i: (i // nt, 0, 0)),
            pl.BlockSpec((nkv, d), lambda i: (0, 0)),
        ],
        out_specs=pl.BlockSpec((tm, d), lambda i: (i, 0)),
        compiler_params=_params("parallel"),
        name="cross_block",
    )(x, g.reshape(1, d), wq, kv, wo)


def _norm_kernel(x_ref, g_ref, o_ref):
    o_ref[...] = _rms_rows(x_ref[...], g_ref[...])


def final_norm_rows(x, g):
    m, d = x.shape
    tm = NORM_TM
    return pl.pallas_call(
        _norm_kernel,
        out_shape=jax.ShapeDtypeStruct((m, d), F32),
        grid=(m // tm,),
        in_specs=[pl.BlockSpec((tm, d), lambda i: (i, 0)), pl.BlockSpec((1, d), lambda i: (0, 0))],
        out_specs=pl.BlockSpec((tm, d), lambda i: (i, 0)),
        compiler_params=_params("parallel"),
        name="final_norm",
    )(x, g.reshape(1, d))


def _rope_angles(pos, dim):
    inv = ROPE_THETA ** (-jnp.arange(0, dim, 2, dtype=F32) / dim)
    ang = pos[:, None] * inv[None, :]
    return jnp.concatenate([ang, ang], axis=-1)


def _rope_tables(seq):
    rows = seq // GRID_W
    row = jnp.repeat(jnp.arange(rows, dtype=F32), GRID_W)
    col = jnp.tile(jnp.arange(GRID_W, dtype=F32), rows)
    half = HEAD_DIM // 2
    ang = jnp.concatenate([_rope_angles(row, half), _rope_angles(col, half)], axis=-1)
    low = jnp.arange(HEAD_DIM) < half
    cos_a = jnp.cos(ang)[:, A_HEAD_PERM]
    sin_a = jnp.sin(ang)[:, A_HEAD_PERM]
    sin_a = jnp.where(low, -sin_a, sin_a)
    a1 = _rope_angles(jnp.arange(seq, dtype=F32), HEAD_DIM)
    cos_b = jnp.cos(a1)
    sin_b = jnp.where(low, -jnp.sin(a1), jnp.sin(a1))
    return (cos_a, sin_a), (cos_b, sin_b)


def _permute_a_heads(a_wqkv, a_q_gain, a_k_gain):
    n_qk = (A_Q_HEADS + A_KV_HEADS) * HEAD_DIM
    cols = np.arange(a_wqkv.shape[-1])
    cols[:n_qk] = (cols[:n_qk].reshape(-1, HEAD_DIM)[:, A_HEAD_PERM]).reshape(-1)
    return a_wqkv[..., cols], a_q_gain[..., A_HEAD_PERM], a_k_gain[..., A_HEAD_PERM]


def _trunk(x, mem, wts, tabs_a, tabs_b, masks):
    bsz, seq, d = x.shape
    mem_len = mem.shape[1]
    x = x.reshape(bsz * seq, d)
    mem = mem.reshape(bsz * mem_len, d)
    depth = wts["norm_mix"].shape[0]
    for i in range(depth):
        j = i // 2
        if i % 2 == 0:
            qk, v = qkv_a(x, wts["norm_mix"][i], wts["a_wqkv"][j], wts["a_q_gain"][j], wts["a_k_gain"][j],
                          tabs_a[0], tabs_a[1], seq)
            o = flash_a(qk.reshape(bsz, seq, -1), v.reshape(bsz, seq, -1), wts["a_q_gain"][j], wts["a_k_gain"][j])
            x = mm_res(o.reshape(bsz * seq, -1), wts["a_wo"][j], x)
        else:
            qkv = qkv_b(x, wts["norm_mix"][i], wts["b_wqkv"][j], tabs_b[0], tabs_b[1], seq)
            o = band_b(qkv.reshape(bsz, seq, -1), masks)
            x = mm_res(o.reshape(bsz * seq, -1), wts["b_wo"][j], x)
        kv = mm_norm(mem, wts["norm_mem"][i], wts["c_wkv"][i], out_dtype=BF16)
        x = cross_block(x, wts["norm_cross"][i], wts["c_wq"][i], kv.reshape(bsz, mem_len, -1), wts["c_wo"][i], seq)
        h = mm_norm(x, wts["norm_mlp"][i], wts["w_up"][i], relu2=True, out_dtype=BF16)
        x = mm_res(h, wts["w_down"][i], x)
    return final_norm_rows(x, wts["final_norm"]).reshape(bsz, seq, d)


def kernel(x_prompt, x_sample, mem_prompt, mem_sample, norm_mix, a_wqkv, a_q_gain, a_k_gain, a_wo, b_wqkv, b_wo, norm_cross, norm_mem, c_wq, c_wkv, c_wo, norm_mlp, w_up, w_down, final_norm):
    a_wqkv, a_q_gain, a_k_gain = _permute_a_heads(a_wqkv, a_q_gain, a_k_gain)
    wts = dict(
        norm_mix=norm_mix, a_wqkv=a_wqkv.astype(BF16), a_q_gain=a_q_gain, a_k_gain=a_k_gain,
        a_wo=a_wo.astype(BF16), b_wqkv=b_wqkv.astype(BF16), b_wo=b_wo.astype(BF16),
        norm_cross=norm_cross, norm_mem=norm_mem, c_wq=c_wq.astype(BF16), c_wkv=c_wkv.astype(BF16),
        c_wo=c_wo.astype(BF16), norm_mlp=norm_mlp, w_up=w_up.astype(BF16), w_down=w_down.astype(BF16),
        final_norm=final_norm,
    )
    masks = jnp.asarray(_band_mask())
    outs = []
    for x, mem in ((x_prompt, mem_prompt), (x_sample, mem_sample)):
        tabs_a, tabs_b = _rope_tables(x.shape[1])
        outs.append(_trunk(x, mem, wts, tabs_a, tabs_b, masks))
    return tuple(outs)
```

```python
import functools

import numpy as np
import jax
import jax.numpy as jnp
from jax import lax
from jax.experimental import pallas as pl
from jax.experimental.pallas import tpu as pltpu

HEAD_DIM = 128
A_Q_HEADS = 16
A_KV_HEADS = 4
A_GROUP = A_Q_HEADS // A_KV_HEADS
B_GROUPS = ((128, 1), (512, 4), (2048, 16))
N_B_GROUPS = 3
B_HEADS = 8
C_HEADS = 4
GRID_W = 64
ROPE_THETA = 10000.0
EPS = 1e-6
ATTN_SCALE = HEAD_DIM ** -0.5
LOG2E = 1.4426950408889634
NEG_BIG = -1e30
BOUNDED_SCORE_LIMIT = 60.0

F32 = jnp.float32
BF16 = jnp.bfloat16

A_HEAD_PERM = np.concatenate([np.arange(0, 32), np.arange(64, 96), np.arange(32, 64), np.arange(96, 128)])

VMEM_LIMIT_BYTES = 56 * 1024 * 1024

MM_TM = 1024
MM_TN = 1024
MM_TK = 2048
FLASH_TQ = 256
FLASH_TK = 1024
BAND_TILE = 2048
BAND_Q = 128
BAND_SIDE = 64
BAND_MIN_HALO = 256
CROSS_TM = 512


def _params(*sem):
    return pltpu.CompilerParams(dimension_semantics=sem, vmem_limit_bytes=VMEM_LIMIT_BYTES)


def _rms_rows(x, g):
    ms = jnp.mean(x * x, axis=-1, keepdims=True)
    return x * lax.rsqrt(ms + EPS) * g


def _layer_spec(block, layer, imap):
    return pl.BlockSpec((None,) + block, lambda *idx: (layer,) + imap(*idx))


def _mm_norm_kernel(x_ref, g_ref, w_ref, o_ref, xn_ref, *, relu2):
    @pl.when(pl.program_id(1) == 0)
    def _():
        xn_ref[...] = _rms_rows(x_ref[...], g_ref[...]).astype(BF16)

    acc = jnp.dot(xn_ref[...], w_ref[...], preferred_element_type=F32)
    if relu2:
        acc = jnp.square(jnp.maximum(acc, 0.0))
    o_ref[...] = acc.astype(o_ref.dtype)


def mm_norm(x, g, w, layer, *, relu2=False, out_dtype=F32):
    m, k = x.shape
    n = w.shape[-1]
    tm = min(MM_TM, m)
    tn = min(MM_TN, n)
    return pl.pallas_call(
        functools.partial(_mm_norm_kernel, relu2=relu2),
        out_shape=jax.ShapeDtypeStruct((m, n), out_dtype),
        grid=(m // tm, n // tn),
        in_specs=[
            pl.BlockSpec((tm, k), lambda i, j: (i, 0)),
            pl.BlockSpec((1, k), lambda i, j: (0, 0)),
            _layer_spec((k, tn), layer, lambda i, j: (0, j)),
        ],
        out_specs=pl.BlockSpec((tm, tn), lambda i, j: (i, j)),
        scratch_shapes=[pltpu.VMEM((tm, k), BF16)],
        compiler_params=_params("parallel", "arbitrary"),
        name="mm_norm",
    )(x, g.reshape(1, k), w)


def _mm_res_kernel(a_ref, w_ref, r_ref, *rest, normed):
    o_ref = rest[-1]

    @pl.when(pl.program_id(2) == 0)
    def _():
        o_ref[...] = r_ref[...]

    o_ref[...] += jnp.dot(a_ref[...], w_ref[...], preferred_element_type=F32)

    if normed:
        @pl.when(pl.program_id(2) == pl.num_programs(2) - 1)
        def _():
            o_ref[...] = _rms_rows(o_ref[...], rest[0][...])


def mm_res(a, w, layer, res, norm_gain=None):
    m, k = a.shape
    n = w.shape[-1]
    normed = norm_gain is not None
    tm = min(MM_TM // 2, m) if normed else min(MM_TM, m)
    tn = n if normed else min(MM_TN, n)
    tk = min(MM_TK, k)
    in_specs = [
        pl.BlockSpec((tm, tk), lambda i, j, kk: (i, kk)),
        _layer_spec((tk, tn), layer, lambda i, j, kk: (kk, j)),
        pl.BlockSpec((tm, tn), lambda i, j, kk: (i, j)),
    ]
    args = [a, w, res]
    if normed:
        in_specs.append(pl.BlockSpec((1, n), lambda i, j, kk: (0, 0)))
        args.append(norm_gain.reshape(1, n))
    return pl.pallas_call(
        functools.partial(_mm_res_kernel, normed=normed),
        out_shape=jax.ShapeDtypeStruct((m, n), F32),
        grid=(m // tm, n // tn, k // tk),
        in_specs=in_specs,
        out_specs=pl.BlockSpec((tm, tn), lambda i, j, kk: (i, j)),
        compiler_params=_params("parallel", "parallel", "arbitrary"),
        name="mm_res_norm" if normed else "mm_res",
    )(*args)


def _rope_half(x, cos, sin_signed):
    return x * cos + pltpu.roll(x, HEAD_DIM // 2, 1) * sin_signed


def _qkv_a_kernel(x_ref, g_ref, w_ref, qg_ref, kg_ref, cos_ref, sin_ref, qk_ref, v_ref, xn_ref):
    j = pl.program_id(1)
    n_q_tiles = A_Q_HEADS // A_KV_HEADS

    @pl.when(j == 0)
    def _():
        xn_ref[...] = _rms_rows(x_ref[...], g_ref[...]).astype(BF16)

    acc = jnp.dot(xn_ref[...], w_ref[...], preferred_element_type=F32)

    @pl.when(j <= n_q_tiles)
    def _():
        gain = jnp.where(j < n_q_tiles, qg_ref[...] * (ATTN_SCALE * LOG2E), kg_ref[...])
        cos = cos_ref[...]
        sin = sin_ref[...]
        heads = []
        for h in range(A_KV_HEADS):
            sl = slice(h * HEAD_DIM, (h + 1) * HEAD_DIM)
            heads.append(_rope_half(_rms_rows(acc[:, sl], gain), cos, sin).astype(BF16))
        qk_ref[...] = jnp.concatenate(heads, axis=1)

    @pl.when(j == n_q_tiles + 1)
    def _():
        ones = jnp.ones((v_ref.shape[0], HEAD_DIM), BF16)
        for h in range(A_KV_HEADS):
            v_ref[:, 2 * h * HEAD_DIM:(2 * h + 1) * HEAD_DIM] = acc[:, h * HEAD_DIM:(h + 1) * HEAD_DIM].astype(BF16)
            v_ref[:, (2 * h + 1) * HEAD_DIM:(2 * h + 2) * HEAD_DIM] = ones


def qkv_a(x, g, w, layer, q_gain, k_gain, cos, sin_signed, seq):
    m, k = x.shape
    nkv = A_KV_HEADS * HEAD_DIM
    tm = MM_TM
    tn = nkv
    nt = seq // tm
    n_qk_tiles = (A_Q_HEADS + A_KV_HEADS) * HEAD_DIM // tn
    tab_spec = pl.BlockSpec((tm, HEAD_DIM), lambda i, j: (i % nt, 0))
    g_spec = pl.BlockSpec((1, HEAD_DIM), lambda i, j: (0, 0))
    return pl.pallas_call(
        _qkv_a_kernel,
        out_shape=(
            jax.ShapeDtypeStruct((m, n_qk_tiles * tn), BF16),
            jax.ShapeDtypeStruct((m, 2 * nkv), BF16),
        ),
        grid=(m // tm, w.shape[-1] // tn),
        in_specs=[
            pl.BlockSpec((tm, k), lambda i, j: (i, 0)),
            pl.BlockSpec((1, k), lambda i, j: (0, 0)),
            _layer_spec((k, tn), layer, lambda i, j: (0, j)),
            g_spec,
            g_spec,
            tab_spec,
            tab_spec,
        ],
        out_specs=(
            pl.BlockSpec((tm, tn), lambda i, j: (i, jnp.minimum(j, n_qk_tiles - 1))),
            pl.BlockSpec((tm, 2 * nkv), lambda i, j: (i, 0)),
        ),
        scratch_shapes=[pltpu.VMEM((tm, k), BF16)],
        compiler_params=_params("parallel", "arbitrary"),
        name="qkv_a",
    )(x, g.reshape(1, k), w, q_gain.reshape(1, HEAD_DIM), k_gain.reshape(1, HEAD_DIM), cos, sin_signed)


def _stack_q(q_ref, qs_ref, tq):
    for g in range(A_GROUP):
        qs_ref[g * tq:(g + 1) * tq, :] = q_ref[0, :, g * HEAD_DIM:(g + 1) * HEAD_DIM]


def _unstack_o(o, o_ref, tq):
    for g in range(A_GROUP):
        o_ref[0, :, g * HEAD_DIM:(g + 1) * HEAD_DIM] = o[g * tq:(g + 1) * tq, :].astype(o_ref.dtype)


def _flash_a_bounded_kernel(q_ref, k_ref, v_ref, o_ref, qs_ref, acc_ref, *, tq, tk, nk):
    _stack_q(q_ref, qs_ref, tq)
    acc_ref[...] = jnp.zeros(acc_ref.shape, F32)

    def step(c, carry):
        start = pl.multiple_of(c * tk, tk)
        kc = k_ref[0, pl.ds(start, tk), :]
        vc = v_ref[0, pl.ds(start, tk), :]
        s = lax.dot_general(qs_ref[...], kc, (((1,), (1,)), ((), ())), preferred_element_type=F32)
        acc_ref[...] += jnp.dot(jnp.exp2(s).astype(BF16), vc, preferred_element_type=F32)
        return carry

    lax.fori_loop(0, nk, step, 0, unroll=True)
    acc = acc_ref[...]
    _unstack_o(acc[:, :HEAD_DIM] / acc[:, HEAD_DIM:], o_ref, tq)


def _flash_a_online_kernel(q_ref, k_ref, v_ref, o_ref, qs_ref, m_ref, acc_ref, *, tq, tk, nk):
    _stack_q(q_ref, qs_ref, tq)
    m_ref[...] = jnp.full(m_ref.shape, NEG_BIG, F32)
    acc_ref[...] = jnp.zeros(acc_ref.shape, F32)

    def step(c, carry):
        start = pl.multiple_of(c * tk, tk)
        kc = k_ref[0, pl.ds(start, tk), :]
        vc = v_ref[0, pl.ds(start, tk), :]
        s = lax.dot_general(qs_ref[...], kc, (((1,), (1,)), ((), ())), preferred_element_type=F32)
        m_prev = m_ref[...]
        m_new = jnp.maximum(m_prev, jnp.max(s, axis=1, keepdims=True))
        alpha = jnp.exp2(m_prev - m_new)
        p = jnp.exp2(s - m_new)
        acc_ref[...] = alpha * acc_ref[...] + jnp.dot(p.astype(BF16), vc, preferred_element_type=F32)
        m_ref[...] = m_new
        return carry

    lax.fori_loop(0, nk, step, 0)
    acc = acc_ref[...]
    _unstack_o(acc[:, :HEAD_DIM] / acc[:, HEAD_DIM:], o_ref, tq)


def _flash_a_call(qk, v, *, bounded):
    bsz, seq, _ = qk.shape
    nq = A_Q_HEADS * HEAD_DIM
    tq, tk = FLASH_TQ, FLASH_TK
    gw = A_GROUP * HEAD_DIM
    rows = A_GROUP * tq
    scratch = [pltpu.VMEM((rows, HEAD_DIM), BF16)]
    if not bounded:
        scratch.append(pltpu.VMEM((rows, 1), F32))
    scratch.append(pltpu.VMEM((rows, 2 * HEAD_DIM), F32))
    body = _flash_a_bounded_kernel if bounded else _flash_a_online_kernel
    return pl.pallas_call(
        functools.partial(body, tq=tq, tk=tk, nk=seq // tk),
        out_shape=jax.ShapeDtypeStruct((bsz, seq, nq), BF16),
        grid=(bsz, A_KV_HEADS, seq // tq),
        in_specs=[
            pl.BlockSpec((1, tq, gw), lambda b, h, i: (b, i, h)),
            pl.BlockSpec((1, seq, HEAD_DIM), lambda b, h, i: (b, 0, A_Q_HEADS + h)),
            pl.BlockSpec((1, seq, 2 * HEAD_DIM), lambda b, h, i: (b, 0, h)),
        ],
        out_specs=pl.BlockSpec((1, tq, gw), lambda b, h, i: (b, i, h)),
        scratch_shapes=scratch,
        compiler_params=_params("parallel", "parallel", "arbitrary"),
        name="flash_a_bounded" if bounded else "flash_a_online",
    )(qk, qk, v)


def flash_a(qk, v, q_gain, k_gain):
    bound = HEAD_DIM * ATTN_SCALE * jnp.max(jnp.abs(q_gain)) * jnp.max(jnp.abs(k_gain))
    return lax.cond(
        bound <= BOUNDED_SCORE_LIMIT,
        functools.partial(_flash_a_call, bounded=True),
        functools.partial(_flash_a_call, bounded=False),
        qk, v,
    )


def _qkv_b_kernel(x_ref, g_ref, w_ref, cos_ref, sin_ref, o_ref, xn_ref):
    j = pl.program_id(1)

    @pl.when(j == 0)
    def _():
        xn_ref[...] = _rms_rows(x_ref[...], g_ref[...]).astype(BF16)

    acc = jnp.dot(xn_ref[...], w_ref[...], preferred_element_type=F32)

    rotary = j < 2 * N_B_GROUPS
    scale = jnp.where(j < N_B_GROUPS, ATTN_SCALE * LOG2E, 1.0).astype(F32)
    cos = jnp.where(rotary, cos_ref[...] * scale, 1.0)
    sin = jnp.where(rotary, sin_ref[...] * scale, 0.0)
    for h in range(B_HEADS):
        sl = slice(h * HEAD_DIM, (h + 1) * HEAD_DIM)
        o_ref[:, sl] = _rope_half(acc[:, sl], cos, sin).astype(BF16)


def qkv_b(x, g, w, layer, cos, sin_signed, seq):
    m, k = x.shape
    n = w.shape[-1]
    tm = MM_TM
    tn = B_HEADS * HEAD_DIM
    nt = seq // tm
    tab_spec = pl.BlockSpec((tm, HEAD_DIM), lambda i, j: (i % nt, 0))
    return pl.pallas_call(
        _qkv_b_kernel,
        out_shape=jax.ShapeDtypeStruct((m, n), BF16),
        grid=(m // tm, n // tn),
        in_specs=[
            pl.BlockSpec((tm, k), lambda i, j: (i, 0)),
            pl.BlockSpec((1, k), lambda i, j: (0, 0)),
            _layer_spec((k, tn), layer, lambda i, j: (0, j)),
            tab_spec,
            tab_spec,
        ],
        out_specs=pl.BlockSpec((tm, tn), lambda i, j: (i, j)),
        scratch_shapes=[pltpu.VMEM((tm, k), BF16)],
        compiler_params=_params("parallel", "arbitrary"),
        name="qkv_b",
    )(x, g.reshape(1, k), w, cos, sin_signed)


def _band_mask():
    rel = np.arange(2 * BAND_Q)[None, :] - np.arange(BAND_Q)[:, None]
    return np.where((rel >= 0) & (rel <= BAND_Q), 0.0, NEG_BIG).astype(np.float32)


def _band_halo(dil):
    return max(BAND_SIDE * dil, BAND_MIN_HALO)


def _rows(start, size, stride):
    return pl.ds(start, size) if stride == 1 else pl.ds(start, size, stride=stride)


def _band_b_kernel(*refs):
    ng = N_B_GROUPS
    q_refs = refs[0:ng]
    k_refs = refs[ng:4 * ng]
    v_refs = refs[4 * ng:7 * ng]
    mask_ref = refs[7 * ng]
    o_ref = refs[7 * ng + 1]
    scratch = refs[7 * ng + 2:]
    dec = [g for g, (_, dil) in enumerate(B_GROUPS) if dil > 1]
    nd = len(dec)
    qf, kf, vf = (dict(zip(dec, scratch[n * nd:(n + 1) * nd])) for n in range(3))
    og, lg, mg = (scratch[3 * nd + n * ng:3 * nd + (n + 1) * ng] for n in range(3))
    tile = BAND_TILE
    i = pl.program_id(1)

    def window(pieces, lo, h):
        hi = lo + 2 * BAND_Q
        parts = []
        if lo < 0:
            parts.append(pieces[0][0, h + lo:h, :])
        parts.append(pieces[1][0, max(lo, 0):min(hi, tile), :])
        if hi > tile:
            parts.append(pieces[2][0, 0:hi - tile, :])
        return parts[0] if len(parts) == 1 else jnp.concatenate(parts, axis=0)

    for g in dec:
        dil = B_GROUPS[g][1]
        h = _band_halo(dil)
        qf[g][...] = q_refs[g][0].astype(F32)
        for src, dst in ((k_refs, kf[g]), (v_refs, vf[g])):
            dst[0:h, :] = src[3 * g][0].astype(F32)
            dst[h:h + tile, :] = src[3 * g + 1][0].astype(F32)
            dst[h + tile:, :] = src[3 * g + 2][0].astype(F32)

    col = lax.broadcasted_iota(jnp.int32, (1, 2 * BAND_Q), 1)
    lo_row = jnp.where((i == 0) & (col < BAND_SIDE), NEG_BIG, 0.0).astype(F32)
    hi_row = jnp.where((i == pl.num_programs(1) - 1) & (col >= 2 * BAND_Q - BAND_SIDE), NEG_BIG, 0.0).astype(F32)
    mask = mask_ref[...]
    masks = {(False, False): mask, (True, False): mask + lo_row,
             (False, True): mask + hi_row, (True, True): mask + lo_row + hi_row}
    ones = jnp.ones((2 * BAND_Q, HEAD_DIM), BF16)

    for g, (_, dil) in enumerate(B_GROUPS):
        h = _band_halo(dil)
        n_sub = tile // (BAND_Q * dil)
        for mb in range(n_sub):
            for r in range(dil):
                q_rows = _rows(dil * BAND_Q * mb + r, BAND_Q, dil)
                if dil == 1:
                    qb = q_refs[g][0, q_rows, :]
                    kb = window(k_refs[3 * g:3 * g + 3], BAND_Q * mb - BAND_SIDE, h)
                    vb = window(v_refs[3 * g:3 * g + 3], BAND_Q * mb - BAND_SIDE, h)
                else:
                    k_rows = _rows(h + dil * (BAND_Q * mb - BAND_SIDE) + r, 2 * BAND_Q, dil)
                    qb = qf[g][q_rows, :].astype(BF16)
                    kb = kf[g][k_rows, :].astype(BF16)
                    vb = vf[g][k_rows, :].astype(BF16)
                vb = jnp.concatenate([vb, ones], axis=1)
                s = lax.dot_general(qb, kb, (((1,), (1,)), ((), ())), preferred_element_type=F32)
                s = s + masks[(mb == 0, mb == n_sub - 1)]
                m = s.max(axis=1, keepdims=True)
                o = jnp.dot(jnp.exp2(s - m).astype(BF16), vb, preferred_element_type=F32)
                og[g][q_rows, :] = o[:, :HEAD_DIM]
                lg[g][q_rows, :] = o[:, HEAD_DIM:]
                mg[g][q_rows, :] = jnp.broadcast_to(m, (BAND_Q, HEAD_DIM))

    chunk = 2 * BAND_Q
    for c in range(tile // chunk):
        rows = slice(c * chunk, (c + 1) * chunk)
        ms = [mg[g][rows, :] for g in range(ng)]
        m = functools.reduce(jnp.maximum, ms)
        ws = [jnp.exp2(mi - m) for mi in ms]
        num = sum(ws[g] * og[g][rows, :] for g in range(ng))
        den = sum(ws[g] * lg[g][rows, :] for g in range(ng))
        o_ref[0, rows, :] = (num / den).astype(o_ref.dtype)


def band_b(qkv, mask):
    bsz, seq, _ = qkv.shape
    tile = BAND_TILE
    nh = N_B_GROUPS * B_HEADS
    assert all(win == 2 * BAND_SIDE * dil for win, dil in B_GROUPS)
    halos = [_band_halo(dil) for _, dil in B_GROUPS]
    assert all(tile % h == 0 and tile % (BAND_Q * dil) == 0 for h, (_, dil) in zip(halos, B_GROUPS))

    def col(which, g, h):
        return which * nh + g * B_HEADS + h

    def tile_spec(which, g):
        return pl.BlockSpec((1, tile, HEAD_DIM), lambda b, i, h: (b, i, col(which, g, h)))

    def halo_spec(which, g, side):
        per_tile = tile // halos[g]
        last = seq // halos[g] - 1

        def imap(b, i, h):
            blk = i * per_tile - 1 if side < 0 else (i + 1) * per_tile
            return (b, jnp.clip(blk, 0, last), col(which, g, h))
        return pl.BlockSpec((1, halos[g], HEAD_DIM), imap)

    def kv_specs(which):
        return [s for g in range(N_B_GROUPS)
                for s in (halo_spec(which, g, -1), tile_spec(which, g), halo_spec(which, g, +1))]

    in_specs = [tile_spec(0, g) for g in range(N_B_GROUPS)] + kv_specs(1) + kv_specs(2)
    in_specs += [pl.BlockSpec(mask.shape, lambda b, i, h: (0, 0))]
    n_in = len(in_specs) - 1
    dec_halos = [h for h, (_, dil) in zip(halos, B_GROUPS) if dil > 1]
    scratch = [pltpu.VMEM((tile, HEAD_DIM), F32) for _ in dec_halos]
    scratch += [pltpu.VMEM((tile + 2 * h, HEAD_DIM), F32) for _ in range(2) for h in dec_halos]
    scratch += [pltpu.VMEM((tile, HEAD_DIM), F32) for _ in range(3 * N_B_GROUPS)]
    return pl.pallas_call(
        _band_b_kernel,
        out_shape=jax.ShapeDtypeStruct((bsz, seq, B_HEADS * HEAD_DIM), BF16),
        grid=(bsz, seq // tile, B_HEADS),
        in_specs=in_specs,
        out_specs=pl.BlockSpec((1, tile, HEAD_DIM), lambda b, i, h: (b, i, h)),
        scratch_shapes=scratch,
        compiler_params=_params("parallel", "parallel", "arbitrary"),
        name="band_b",
    )(*([qkv] * n_in), mask)


def _cross_kernel(x_ref, g_ref, wq_ref, kv_ref, wo_ref, o_ref):
    x = x_ref[...]
    xn = _rms_rows(x, g_ref[...]).astype(BF16)
    q = (jnp.dot(xn, wq_ref[...], preferred_element_type=F32) * ATTN_SCALE).astype(BF16)
    nkv = C_HEADS * HEAD_DIM
    outs = []
    for h in range(C_HEADS):
        sl = slice(h * HEAD_DIM, (h + 1) * HEAD_DIM)
        kh = kv_ref[0, :, sl]
        vh = kv_ref[0, :, nkv + h * HEAD_DIM:nkv + (h + 1) * HEAD_DIM]
        s = lax.dot_general(q[:, sl], kh, (((1,), (1,)), ((), ())), preferred_element_type=F32)
        p = jnp.exp(s - s.max(axis=1, keepdims=True))
        l = p.sum(axis=1, keepdims=True)
        outs.append((jnp.dot(p.astype(BF16), vh, preferred_element_type=F32) / l).astype(BF16))
    o = jnp.concatenate(outs, axis=1)
    o_ref[...] = x + jnp.dot(o, wo_ref[...], preferred_element_type=F32)


def cross_block(x, g, wq, kv, wo, layer, seq):
    m, d = x.shape
    tm = CROSS_TM
    nt = seq // tm
    nkv = C_HEADS * HEAD_DIM
    mem_len = kv.shape[1]
    return pl.pallas_call(
        _cross_kernel,
        out_shape=jax.ShapeDtypeStruct((m, d), F32),
        grid=(m // tm,),
        in_specs=[
            pl.BlockSpec((tm, d), lambda i: (i, 0)),
            pl.BlockSpec((1, d), lambda i: (0, 0)),
            _layer_spec((d, nkv), layer, lambda i: (0, 0)),
            pl.BlockSpec((1, mem_len, 2 * nkv), lambda i: (i // nt, 0, 0)),
            _layer_spec((nkv, d), layer, lambda i: (0, 0)),
        ],
        out_specs=pl.BlockSpec((tm, d), lambda i: (i, 0)),
        compiler_params=_params("parallel"),
        name="cross_block",
    )(x, g.reshape(1, d), wq, kv, wo)


def _rope_angles(pos, dim):
    inv = ROPE_THETA ** (-jnp.arange(0, dim, 2, dtype=F32) / dim)
    ang = pos[:, None] * inv[None, :]
    return jnp.concatenate([ang, ang], axis=-1)


def _rope_tables(seq):
    rows = seq // GRID_W
    row = jnp.repeat(jnp.arange(rows, dtype=F32), GRID_W)
    col = jnp.tile(jnp.arange(GRID_W, dtype=F32), rows)
    half = HEAD_DIM // 2
    ang = jnp.concatenate([_rope_angles(row, half), _rope_angles(col, half)], axis=-1)
    low = jnp.arange(HEAD_DIM) < half
    cos_a = jnp.cos(ang)[:, A_HEAD_PERM]
    sin_a = jnp.sin(ang)[:, A_HEAD_PERM]
    sin_a = jnp.where(low, -sin_a, sin_a)
    a1 = _rope_angles(jnp.arange(seq, dtype=F32), HEAD_DIM)
    cos_b = jnp.cos(a1)
    sin_b = jnp.where(low, -jnp.sin(a1), jnp.sin(a1))
    return (cos_a, sin_a), (cos_b, sin_b)


def _permute_a_heads(a_wqkv, a_q_gain, a_k_gain):
    n_qk = (A_Q_HEADS + A_KV_HEADS) * HEAD_DIM
    cols = np.arange(a_wqkv.shape[-1])
    cols[:n_qk] = (cols[:n_qk].reshape(-1, HEAD_DIM)[:, A_HEAD_PERM]).reshape(-1)
    return a_wqkv[..., cols], a_q_gain[..., A_HEAD_PERM], a_k_gain[..., A_HEAD_PERM]


def _trunk(x, mem, wts, tabs_a, tabs_b, masks):
    bsz, seq, d = x.shape
    mem_len = mem.shape[1]
    x = x.reshape(bsz * seq, d)
    mem = mem.reshape(bsz * mem_len, d)
    depth = wts["norm_mix"].shape[0]
    for i in range(depth):
        j = i // 2
        if i % 2 == 0:
            qk, v = qkv_a(x, wts["norm_mix"][i], wts["a_wqkv"], j, wts["a_q_gain"][j], wts["a_k_gain"][j],
                          tabs_a[0], tabs_a[1], seq)
            o = flash_a(qk.reshape(bsz, seq, -1), v.reshape(bsz, seq, -1), wts["a_q_gain"][j], wts["a_k_gain"][j])
            x = mm_res(o.reshape(bsz * seq, -1), wts["a_wo"], j, x)
        else:
            qkv = qkv_b(x, wts["norm_mix"][i], wts["b_wqkv"], j, tabs_b[0], tabs_b[1], seq)
            o = band_b(qkv.reshape(bsz, seq, -1), masks)
            x = mm_res(o.reshape(bsz * seq, -1), wts["b_wo"], j, x)
        kv = mm_norm(mem, wts["norm_mem"][i], wts["c_wkv"], i, out_dtype=BF16)
        x = cross_block(x, wts["norm_cross"][i], wts["c_wq"], kv.reshape(bsz, mem_len, -1), wts["c_wo"], i, seq)
        h = mm_norm(x, wts["norm_mlp"][i], wts["w_up"], i, relu2=True, out_dtype=BF16)
        x = mm_res(h, wts["w_down"], i, x, norm_gain=wts["final_norm"] if i == depth - 1 else None)
    return x.reshape(bsz, seq, d)


def kernel(x_prompt, x_sample, mem_prompt, mem_sample, norm_mix, a_wqkv, a_q_gain, a_k_gain, a_wo, b_wqkv, b_wo, norm_cross, norm_mem, c_wq, c_wkv, c_wo, norm_mlp, w_up, w_down, final_norm):
    a_wqkv, a_q_gain, a_k_gain = _permute_a_heads(a_wqkv, a_q_gain, a_k_gain)
    wts = dict(
        norm_mix=norm_mix, a_wqkv=a_wqkv.astype(BF16), a_q_gain=a_q_gain, a_k_gain=a_k_gain,
        a_wo=a_wo.astype(BF16), b_wqkv=b_wqkv.astype(BF16), b_wo=b_wo.astype(BF16),
        norm_cross=norm_cross, norm_mem=norm_mem, c_wq=c_wq.astype(BF16), c_wkv=c_wkv.astype(BF16),
        c_wo=c_wo.astype(BF16), norm_mlp=norm_mlp, w_up=w_up.astype(BF16), w_down=w_down.astype(BF16),
        final_norm=final_norm,
    )
    masks = jnp.asarray(_band_mask())
    outs = []
    for x, mem in ((x_prompt, mem_prompt), (x_sample, mem_sample)):
        tabs_a, tabs_b = _rope_tables(x.shape[1])
        outs.append(_trunk(x, mem, wts, tabs_a, tabs_b, masks))
    return tuple(outs)
```

```python
import functools

import numpy as np
import jax
import jax.numpy as jnp
from jax import lax
from jax.experimental import pallas as pl
from jax.experimental.pallas import tpu as pltpu

HEAD_DIM = 128
A_Q_HEADS = 16
A_KV_HEADS = 4
A_GROUP = A_Q_HEADS // A_KV_HEADS
B_GROUPS = ((128, 1), (512, 4), (2048, 16))
N_B_GROUPS = 3
B_HEADS = 8
C_HEADS = 4
GRID_W = 64
ROPE_THETA = 10000.0
EPS = 1e-6
ATTN_SCALE = HEAD_DIM ** -0.5
LOG2E = 1.4426950408889634
NEG_BIG = -1e30
BOUNDED_SCORE_LIMIT = 60.0

F32 = jnp.float32
BF16 = jnp.bfloat16

A_HEAD_PERM = np.concatenate([np.arange(0, 32), np.arange(64, 96), np.arange(32, 64), np.arange(96, 128)])

VMEM_LIMIT_BYTES = 56 * 1024 * 1024

MM_TM = 1024
MM_TN = 1024
MM_TK = 2048
QKV_B_TN = 1536
FLASH_TQ = 256
FLASH_TK = 1024
BAND_TILE = 2048
BAND_Q = 128
BAND_SIDE = 64
BAND_MIN_HALO = 256
CROSS_TM = 512


def _params(*sem):
    return pltpu.CompilerParams(dimension_semantics=sem, vmem_limit_bytes=VMEM_LIMIT_BYTES)


def _rms_rows(x, g):
    ms = jnp.mean(x * x, axis=-1, keepdims=True)
    return x * lax.rsqrt(ms + EPS) * g


def _layer_spec(block, layer, imap):
    return pl.BlockSpec((None,) + block, lambda *idx: (layer,) + imap(*idx))


def _mm_norm_kernel(x_ref, g_ref, w_ref, o_ref, xn_ref, *, relu2):
    @pl.when(pl.program_id(1) == 0)
    def _():
        xn_ref[...] = _rms_rows(x_ref[...], g_ref[...]).astype(BF16)

    acc = jnp.dot(xn_ref[...], w_ref[...], preferred_element_type=F32)
    if relu2:
        acc = jnp.square(jnp.maximum(acc, 0.0))
    o_ref[...] = acc.astype(o_ref.dtype)


def mm_norm(x, g, w, layer, *, relu2=False, out_dtype=F32):
    m, k = x.shape
    n = w.shape[-1]
    tm = min(MM_TM, m)
    tn = min(MM_TN, n)
    return pl.pallas_call(
        functools.partial(_mm_norm_kernel, relu2=relu2),
        out_shape=jax.ShapeDtypeStruct((m, n), out_dtype),
        grid=(m // tm, n // tn),
        in_specs=[
            pl.BlockSpec((tm, k), lambda i, j: (i, 0)),
            pl.BlockSpec((1, k), lambda i, j: (0, 0)),
            _layer_spec((k, tn), layer, lambda i, j: (0, j)),
        ],
        out_specs=pl.BlockSpec((tm, tn), lambda i, j: (i, j)),
        scratch_shapes=[pltpu.VMEM((tm, k), BF16)],
        compiler_params=_params("parallel", "arbitrary"),
        name="mm_norm",
    )(x, g.reshape(1, k), w)


def _mm_res_kernel(a_ref, w_ref, r_ref, *rest, normed):
    o_ref = rest[-1]

    @pl.when(pl.program_id(2) == 0)
    def _():
        o_ref[...] = r_ref[...]

    o_ref[...] += jnp.dot(a_ref[...], w_ref[...], preferred_element_type=F32)

    if normed:
        @pl.when(pl.program_id(2) == pl.num_programs(2) - 1)
        def _():
            o_ref[...] = _rms_rows(o_ref[...], rest[0][...])


def mm_res(a, w, layer, res, norm_gain=None):
    m, k = a.shape
    n = w.shape[-1]
    normed = norm_gain is not None
    tm = min(MM_TM // 2, m) if normed else min(MM_TM, m)
    tn = n if normed else min(MM_TN, n)
    tk = min(MM_TK, k)
    in_specs = [
        pl.BlockSpec((tm, tk), lambda i, j, kk: (i, kk)),
        _layer_spec((tk, tn), layer, lambda i, j, kk: (kk, j)),
        pl.BlockSpec((tm, tn), lambda i, j, kk: (i, j)),
    ]
    args = [a, w, res]
    if normed:
        in_specs.append(pl.BlockSpec((1, n), lambda i, j, kk: (0, 0)))
        args.append(norm_gain.reshape(1, n))
    return pl.pallas_call(
        functools.partial(_mm_res_kernel, normed=normed),
        out_shape=jax.ShapeDtypeStruct((m, n), F32),
        grid=(m // tm, n // tn, k // tk),
        in_specs=in_specs,
        out_specs=pl.BlockSpec((tm, tn), lambda i, j, kk: (i, j)),
        compiler_params=_params("parallel", "parallel", "arbitrary"),
        name="mm_res_norm" if normed else "mm_res",
    )(*args)


def _rope_half(x, cos, sin_signed):
    return x * cos + pltpu.roll(x, HEAD_DIM // 2, 1) * sin_signed


def _qkv_a_kernel(x_ref, g_ref, w_ref, qg_ref, kg_ref, cos_ref, sin_ref, qk_ref, v_ref, xn_ref):
    j = pl.program_id(1)
    n_q_tiles = A_Q_HEADS // A_KV_HEADS

    @pl.when(j == 0)
    def _():
        xn_ref[...] = _rms_rows(x_ref[...], g_ref[...]).astype(BF16)

    acc = jnp.dot(xn_ref[...], w_ref[...], preferred_element_type=F32)

    @pl.when(j <= n_q_tiles)
    def _():
        gain = jnp.where(j < n_q_tiles, qg_ref[...] * (ATTN_SCALE * LOG2E), kg_ref[...])
        cos = cos_ref[...]
        sin = sin_ref[...]
        heads = []
        for h in range(A_KV_HEADS):
            sl = slice(h * HEAD_DIM, (h + 1) * HEAD_DIM)
            heads.append(_rope_half(_rms_rows(acc[:, sl], gain), cos, sin).astype(BF16))
        qk_ref[...] = jnp.concatenate(heads, axis=1)

    @pl.when(j == n_q_tiles + 1)
    def _():
        ones = jnp.ones((v_ref.shape[0], HEAD_DIM), BF16)
        for h in range(A_KV_HEADS):
            v_ref[:, 2 * h * HEAD_DIM:(2 * h + 1) * HEAD_DIM] = acc[:, h * HEAD_DIM:(h + 1) * HEAD_DIM].astype(BF16)
            v_ref[:, (2 * h + 1) * HEAD_DIM:(2 * h + 2) * HEAD_DIM] = ones


def qkv_a(x, g, w, layer, q_gain, k_gain, cos, sin_signed, seq):
    m, k = x.shape
    nkv = A_KV_HEADS * HEAD_DIM
    tm = MM_TM
    tn = nkv
    nt = seq // tm
    n_qk_tiles = (A_Q_HEADS + A_KV_HEADS) * HEAD_DIM // tn
    tab_spec = pl.BlockSpec((tm, HEAD_DIM), lambda i, j: (i % nt, 0))
    g_spec = pl.BlockSpec((1, HEAD_DIM), lambda i, j: (0, 0))
    return pl.pallas_call(
        _qkv_a_kernel,
        out_shape=(
            jax.ShapeDtypeStruct((m, n_qk_tiles * tn), BF16),
            jax.ShapeDtypeStruct((m, 2 * nkv), BF16),
        ),
        grid=(m // tm, w.shape[-1] // tn),
        in_specs=[
            pl.BlockSpec((tm, k), lambda i, j: (i, 0)),
            pl.BlockSpec((1, k), lambda i, j: (0, 0)),
            _layer_spec((k, tn), layer, lambda i, j: (0, j)),
            g_spec,
            g_spec,
            tab_spec,
            tab_spec,
        ],
        out_specs=(
            pl.BlockSpec((tm, tn), lambda i, j: (i, jnp.minimum(j, n_qk_tiles - 1))),
            pl.BlockSpec((tm, 2 * nkv), lambda i, j: (i, 0)),
        ),
        scratch_shapes=[pltpu.VMEM((tm, k), BF16)],
        compiler_params=_params("parallel", "arbitrary"),
        name="qkv_a",
    )(x, g.reshape(1, k), w, q_gain.reshape(1, HEAD_DIM), k_gain.reshape(1, HEAD_DIM), cos, sin_signed)


def _stack_q(q_ref, qs_ref, tq):
    for g in range(A_GROUP):
        qs_ref[g * tq:(g + 1) * tq, :] = q_ref[0, :, g * HEAD_DIM:(g + 1) * HEAD_DIM]


def _unstack_o(o, o_ref, tq):
    for g in range(A_GROUP):
        o_ref[0, :, g * HEAD_DIM:(g + 1) * HEAD_DIM] = o[g * tq:(g + 1) * tq, :].astype(o_ref.dtype)


def _flash_a_bounded_kernel(q_ref, k_ref, v_ref, o_ref, qs_ref, acc_ref, *, tq, tk, nk):
    _stack_q(q_ref, qs_ref, tq)
    acc_ref[...] = jnp.zeros(acc_ref.shape, F32)

    def step(c, carry):
        start = pl.multiple_of(c * tk, tk)
        kc = k_ref[0, pl.ds(start, tk), :]
        vc = v_ref[0, pl.ds(start, tk), :]
        s = lax.dot_general(qs_ref[...], kc, (((1,), (1,)), ((), ())), preferred_element_type=F32)
        acc_ref[...] += jnp.dot(jnp.exp2(s).astype(BF16), vc, preferred_element_type=F32)
        return carry

    lax.fori_loop(0, nk, step, 0, unroll=True)
    acc = acc_ref[...]
    _unstack_o(acc[:, :HEAD_DIM] / acc[:, HEAD_DIM:], o_ref, tq)


def _flash_a_online_kernel(q_ref, k_ref, v_ref, o_ref, qs_ref, m_ref, acc_ref, *, tq, tk, nk):
    _stack_q(q_ref, qs_ref, tq)
    m_ref[...] = jnp.full(m_ref.shape, NEG_BIG, F32)
    acc_ref[...] = jnp.zeros(acc_ref.shape, F32)

    def step(c, carry):
        start = pl.multiple_of(c * tk, tk)
        kc = k_ref[0, pl.ds(start, tk), :]
        vc = v_ref[0, pl.ds(start, tk), :]
        s = lax.dot_general(qs_ref[...], kc, (((1,), (1,)), ((), ())), preferred_element_type=F32)
        m_prev = m_ref[...]
        m_new = jnp.maximum(m_prev, jnp.max(s, axis=1, keepdims=True))
        alpha = jnp.exp2(m_prev - m_new)
        p = jnp.exp2(s - m_new)
        acc_ref[...] = alpha * acc_ref[...] + jnp.dot(p.astype(BF16), vc, preferred_element_type=F32)
        m_ref[...] = m_new
        return carry

    lax.fori_loop(0, nk, step, 0)
    acc = acc_ref[...]
    _unstack_o(acc[:, :HEAD_DIM] / acc[:, HEAD_DIM:], o_ref, tq)


def _flash_a_call(qk, v, *, bounded):
    bsz, seq, _ = qk.shape
    nq = A_Q_HEADS * HEAD_DIM
    tq, tk = FLASH_TQ, FLASH_TK
    gw = A_GROUP * HEAD_DIM
    rows = A_GROUP * tq
    scratch = [pltpu.VMEM((rows, HEAD_DIM), BF16)]
    if not bounded:
        scratch.append(pltpu.VMEM((rows, 1), F32))
    scratch.append(pltpu.VMEM((rows, 2 * HEAD_DIM), F32))
    body = _flash_a_bounded_kernel if bounded else _flash_a_online_kernel
    return pl.pallas_call(
        functools.partial(body, tq=tq, tk=tk, nk=seq // tk),
        out_shape=jax.ShapeDtypeStruct((bsz, seq, nq), BF16),
        grid=(bsz, A_KV_HEADS, seq // tq),
        in_specs=[
            pl.BlockSpec((1, tq, gw), lambda b, h, i: (b, i, h)),
            pl.BlockSpec((1, seq, HEAD_DIM), lambda b, h, i: (b, 0, A_Q_HEADS + h)),
            pl.BlockSpec((1, seq, 2 * HEAD_DIM), lambda b, h, i: (b, 0, h)),
        ],
        out_specs=pl.BlockSpec((1, tq, gw), lambda b, h, i: (b, i, h)),
        scratch_shapes=scratch,
        compiler_params=_params("parallel", "parallel", "arbitrary"),
        name="flash_a_bounded" if bounded else "flash_a_online",
    )(qk, qk, v)


def flash_a(qk, v, q_gain, k_gain):
    bound = HEAD_DIM * ATTN_SCALE * jnp.max(jnp.abs(q_gain)) * jnp.max(jnp.abs(k_gain))
    return lax.cond(
        bound <= BOUNDED_SCORE_LIMIT,
        functools.partial(_flash_a_call, bounded=True),
        functools.partial(_flash_a_call, bounded=False),
        qk, v,
    )


def _qkv_b_kernel(x_ref, g_ref, w_ref, cos_ref, sin_ref, o_ref, xn_ref, *, tiles_per_which):
    j = pl.program_id(1)

    @pl.when(j == 0)
    def _():
        xn_ref[...] = _rms_rows(x_ref[...], g_ref[...]).astype(BF16)

    acc = jnp.dot(xn_ref[...], w_ref[...], preferred_element_type=F32)

    rotary = j < 2 * tiles_per_which
    scale = jnp.where(j < tiles_per_which, ATTN_SCALE * LOG2E, 1.0).astype(F32)
    cos = jnp.where(rotary, cos_ref[...] * scale, 1.0)
    sin = jnp.where(rotary, sin_ref[...] * scale, 0.0)
    for h in range(o_ref.shape[1] // HEAD_DIM):
        sl = slice(h * HEAD_DIM, (h + 1) * HEAD_DIM)
        o_ref[:, sl] = _rope_half(acc[:, sl], cos, sin).astype(BF16)


def qkv_b(x, g, w, layer, cos, sin_signed, seq):
    m, k = x.shape
    n = w.shape[-1]
    tm = MM_TM
    tn = QKV_B_TN
    assert (n // 3) % tn == 0
    nt = seq // tm
    tab_spec = pl.BlockSpec((tm, HEAD_DIM), lambda i, j: (i % nt, 0))
    return pl.pallas_call(
        functools.partial(_qkv_b_kernel, tiles_per_which=n // 3 // tn),
        out_shape=jax.ShapeDtypeStruct((m, n), BF16),
        grid=(m // tm, n // tn),
        in_specs=[
            pl.BlockSpec((tm, k), lambda i, j: (i, 0)),
            pl.BlockSpec((1, k), lambda i, j: (0, 0)),
            _layer_spec((k, tn), layer, lambda i, j: (0, j)),
            tab_spec,
            tab_spec,
        ],
        out_specs=pl.BlockSpec((tm, tn), lambda i, j: (i, j)),
        scratch_shapes=[pltpu.VMEM((tm, k), BF16)],
        compiler_params=_params("parallel", "arbitrary"),
        name="qkv_b",
    )(x, g.reshape(1, k), w, cos, sin_signed)


def _band_mask():
    rel = np.arange(2 * BAND_Q)[None, :] - np.arange(BAND_Q)[:, None]
    return np.where((rel >= 0) & (rel <= BAND_Q), 0.0, NEG_BIG).astype(np.float32)


def _band_halo(dil):
    return max(BAND_SIDE * dil, BAND_MIN_HALO)


def _rows(start, size, stride):
    return pl.ds(start, size) if stride == 1 else pl.ds(start, size, stride=stride)


def _band_b_kernel(*refs):
    ng = N_B_GROUPS
    q_refs = refs[0:ng]
    k_refs = refs[ng:4 * ng]
    v_refs = refs[4 * ng:7 * ng]
    mask_ref = refs[7 * ng]
    o_ref = refs[7 * ng + 1]
    scratch = refs[7 * ng + 2:]
    dec = [g for g, (_, dil) in enumerate(B_GROUPS) if dil > 1]
    nd = len(dec)
    qf, kf, vf = (dict(zip(dec, scratch[n * nd:(n + 1) * nd])) for n in range(3))
    og, lg, mg = (scratch[3 * nd + n * ng:3 * nd + (n + 1) * ng] for n in range(3))
    tile = BAND_TILE
    i = pl.program_id(1)

    def window(pieces, lo, h):
        hi = lo + 2 * BAND_Q
        parts = []
        if lo < 0:
            parts.append(pieces[0][0, h + lo:h, :])
        parts.append(pieces[1][0, max(lo, 0):min(hi, tile), :])
        if hi > tile:
            parts.append(pieces[2][0, 0:hi - tile, :])
        return parts[0] if len(parts) == 1 else jnp.concatenate(parts, axis=0)

    for g in dec:
        dil = B_GROUPS[g][1]
        h = _band_halo(dil)
        qf[g][...] = q_refs[g][0].astype(F32)
        for src, dst in ((k_refs, kf[g]), (v_refs, vf[g])):
            dst[0:h, :] = src[3 * g][0].astype(F32)
            dst[h:h + tile, :] = src[3 * g + 1][0].astype(F32)
            dst[h + tile:, :] = src[3 * g + 2][0].astype(F32)

    col = lax.broadcasted_iota(jnp.int32, (1, 2 * BAND_Q), 1)
    lo_row = jnp.where((i == 0) & (col < BAND_SIDE), NEG_BIG, 0.0).astype(F32)
    hi_row = jnp.where((i == pl.num_programs(1) - 1) & (col >= 2 * BAND_Q - BAND_SIDE), NEG_BIG, 0.0).astype(F32)
    mask = mask_ref[...]
    masks = {(False, False): mask, (True, False): mask + lo_row,
             (False, True): mask + hi_row, (True, True): mask + lo_row + hi_row}
    ones = jnp.ones((2 * BAND_Q, HEAD_DIM), BF16)

    for g, (_, dil) in enumerate(B_GROUPS):
        h = _band_halo(dil)
        n_sub = tile // (BAND_Q * dil)
        for mb in range(n_sub):
            for r in range(dil):
                q_rows = _rows(dil * BAND_Q * mb + r, BAND_Q, dil)
                if dil == 1:
                    qb = q_refs[g][0, q_rows, :]
                    kb = window(k_refs[3 * g:3 * g + 3], BAND_Q * mb - BAND_SIDE, h)
                    vb = window(v_refs[3 * g:3 * g + 3], BAND_Q * mb - BAND_SIDE, h)
                else:
                    k_rows = _rows(h + dil * (BAND_Q * mb - BAND_SIDE) + r, 2 * BAND_Q, dil)
                    qb = qf[g][q_rows, :].astype(BF16)
                    kb = kf[g][k_rows, :].astype(BF16)
                    vb = vf[g][k_rows, :].astype(BF16)
                vb = jnp.concatenate([vb, ones], axis=1)
                s = lax.dot_general(qb, kb, (((1,), (1,)), ((), ())), preferred_element_type=F32)
                s = s + masks[(mb == 0, mb == n_sub - 1)]
                m = s.max(axis=1, keepdims=True)
                o = jnp.dot(jnp.exp2(s - m).astype(BF16), vb, preferred_element_type=F32)
                og[g][q_rows, :] = o[:, :HEAD_DIM]
                lg[g][q_rows, :] = o[:, HEAD_DIM:]
                mg[g][q_rows, :] = jnp.broadcast_to(m, (BAND_Q, HEAD_DIM))

    chunk = 2 * BAND_Q
    for c in range(tile // chunk):
        rows = slice(c * chunk, (c + 1) * chunk)
        ms = [mg[g][rows, :] for g in range(ng)]
        m = functools.reduce(jnp.maximum, ms)
        ws = [jnp.exp2(mi - m) for mi in ms]
        num = sum(ws[g] * og[g][rows, :] for g in range(ng))
        den = sum(ws[g] * lg[g][rows, :] for g in range(ng))
        o_ref[0, rows, :] = (num / den).astype(o_ref.dtype)


def band_b(qkv, mask):
    bsz, seq, _ = qkv.shape
    tile = BAND_TILE
    nh = N_B_GROUPS * B_HEADS
    assert all(win == 2 * BAND_SIDE * dil for win, dil in B_GROUPS)
    halos = [_band_halo(dil) for _, dil in B_GROUPS]
    assert all(tile % h == 0 and tile % (BAND_Q * dil) == 0 for h, (_, dil) in zip(halos, B_GROUPS))

    def col(which, g, h):
        return which * nh + g * B_HEADS + h

    def tile_spec(which, g):
        return pl.BlockSpec((1, tile, HEAD_DIM), lambda b, i, h: (b, i, col(which, g, h)))

    def halo_spec(which, g, side):
        per_tile = tile // halos[g]
        last = seq // halos[g] - 1

        def imap(b, i, h):
            blk = i * per_tile - 1 if side < 0 else (i + 1) * per_tile
            return (b, jnp.clip(blk, 0, last), col(which, g, h))
        return pl.BlockSpec((1, halos[g], HEAD_DIM), imap)

    def kv_specs(which):
        return [s for g in range(N_B_GROUPS)
                for s in (halo_spec(which, g, -1), tile_spec(which, g), halo_spec(which, g, +1))]

    in_specs = [tile_spec(0, g) for g in range(N_B_GROUPS)] + kv_specs(1) + kv_specs(2)
    in_specs += [pl.BlockSpec(mask.shape, lambda b, i, h: (0, 0))]
    n_in = len(in_specs) - 1
    dec_halos = [h for h, (_, dil) in zip(halos, B_GROUPS) if dil > 1]
    scratch = [pltpu.VMEM((tile, HEAD_DIM), F32) for _ in dec_halos]
    scratch += [pltpu.VMEM((tile + 2 * h, HEAD_DIM), F32) for _ in range(2) for h in dec_halos]
    scratch += [pltpu.VMEM((tile, HEAD_DIM), F32) for _ in range(3 * N_B_GROUPS)]
    return pl.pallas_call(
        _band_b_kernel,
        out_shape=jax.ShapeDtypeStruct((bsz, seq, B_HEADS * HEAD_DIM), BF16),
        grid=(bsz, seq // tile, B_HEADS),
        in_specs=in_specs,
        out_specs=pl.BlockSpec((1, tile, HEAD_DIM), lambda b, i, h: (b, i, h)),
        scratch_shapes=scratch,
        compiler_params=_params("parallel", "parallel", "arbitrary"),
        name="band_b",
    )(*([qkv] * n_in), mask)


def _cross_kernel(x_ref, g_ref, wq_ref, kv_ref, wo_ref, o_ref):
    x = x_ref[...]
    xn = _rms_rows(x, g_ref[...]).astype(BF16)
    q = (jnp.dot(xn, wq_ref[...], preferred_element_type=F32) * ATTN_SCALE).astype(BF16)
    nkv = C_HEADS * HEAD_DIM
    outs = []
    for h in range(C_HEADS):
        sl = slice(h * HEAD_DIM, (h + 1) * HEAD_DIM)
        kh = kv_ref[0, :, sl]
        vh = kv_ref[0, :, nkv + h * HEAD_DIM:nkv + (h + 1) * HEAD_DIM]
        s = lax.dot_general(q[:, sl], kh, (((1,), (1,)), ((), ())), preferred_element_type=F32)
        p = jnp.exp(s - s.max(axis=1, keepdims=True))
        l = p.sum(axis=1, keepdims=True)
        outs.append((jnp.dot(p.astype(BF16), vh, preferred_element_type=F32) / l).astype(BF16))
    o = jnp.concatenate(outs, axis=1)
    o_ref[...] = x + jnp.dot(o, wo_ref[...], preferred_element_type=F32)


def cross_block(x, g, wq, kv, wo, layer, seq):
    m, d = x.shape
    tm = CROSS_TM
    nt = seq // tm
    nkv = C_HEADS * HEAD_DIM
    mem_len = kv.shape[1]
    return pl.pallas_call(
        _cross_kernel,
        out_shape=jax.ShapeDtypeStruct((m, d), F32),
        grid=(m // tm,),
        in_specs=[
            pl.BlockSpec((tm, d), lambda i: (i, 0)),
            pl.BlockSpec((1, d), lambda i: (0, 0)),
            _layer_spec((d, nkv), layer, lambda i: (0, 0)),
            pl.BlockSpec((1, mem_len, 2 * nkv), lambda i: (i // nt, 0, 0)),
            _layer_spec((nkv, d), layer, lambda i: (0, 0)),
        ],
        out_specs=pl.BlockSpec((tm, d), lambda i: (i, 0)),
        compiler_params=_params("parallel"),
        name="cross_block",
    )(x, g.reshape(1, d), wq, kv, wo)


def _rope_angles(pos, dim):
    inv = ROPE_THETA ** (-jnp.arange(0, dim, 2, dtype=F32) / dim)
    ang = pos[:, None] * inv[None, :]
    return jnp.concatenate([ang, ang], axis=-1)


def _rope_tables(seq):
    rows = seq // GRID_W
    row = jnp.repeat(jnp.arange(rows, dtype=F32), GRID_W)
    col = jnp.tile(jnp.arange(GRID_W, dtype=F32), rows)
    half = HEAD_DIM // 2
    ang = jnp.concatenate([_rope_angles(row, half), _rope_angles(col, half)], axis=-1)
    low = jnp.arange(HEAD_DIM) < half
    cos_a = jnp.cos(ang)[:, A_HEAD_PERM]
    sin_a = jnp.sin(ang)[:, A_HEAD_PERM]
    sin_a = jnp.where(low, -sin_a, sin_a)
    a1 = _rope_angles(jnp.arange(seq, dtype=F32), HEAD_DIM)
    cos_b = jnp.cos(a1)
    sin_b = jnp.where(low, -jnp.sin(a1), jnp.sin(a1))
    return (cos_a, sin_a), (cos_b, sin_b)


def _permute_a_heads(a_wqkv, a_q_gain, a_k_gain):
    n_qk = (A_Q_HEADS + A_KV_HEADS) * HEAD_DIM
    cols = np.arange(a_wqkv.shape[-1])
    cols[:n_qk] = (cols[:n_qk].reshape(-1, HEAD_DIM)[:, A_HEAD_PERM]).reshape(-1)
    return a_wqkv[..., cols], a_q_gain[..., A_HEAD_PERM], a_k_gain[..., A_HEAD_PERM]


def _trunk(x, mem, wts, tabs_a, tabs_b, masks):
    bsz, seq, d = x.shape
    mem_len = mem.shape[1]
    x = x.reshape(bsz * seq, d)
    mem = mem.reshape(bsz * mem_len, d)
    depth = wts["norm_mix"].shape[0]
    for i in range(depth):
        j = i // 2
        if i % 2 == 0:
            qk, v = qkv_a(x, wts["norm_mix"][i], wts["a_wqkv"], j, wts["a_q_gain"][j], wts["a_k_gain"][j],
                          tabs_a[0], tabs_a[1], seq)
            o = flash_a(qk.reshape(bsz, seq, -1), v.reshape(bsz, seq, -1), wts["a_q_gain"][j], wts["a_k_gain"][j])
            x = mm_res(o.reshape(bsz * seq, -1), wts["a_wo"], j, x)
        else:
            qkv = qkv_b(x, wts["norm_mix"][i], wts["b_wqkv"], j, tabs_b[0], tabs_b[1], seq)
            o = band_b(qkv.reshape(bsz, seq, -1), masks)
            x = mm_res(o.reshape(bsz * seq, -1), wts["b_wo"], j, x)
        kv = mm_norm(mem, wts["norm_mem"][i], wts["c_wkv"], i, out_dtype=BF16)
        x = cross_block(x, wts["norm_cross"][i], wts["c_wq"], kv.reshape(bsz, mem_len, -1), wts["c_wo"], i, seq)
        h = mm_norm(x, wts["norm_mlp"][i], wts["w_up"], i, relu2=True, out_dtype=BF16)
        x = mm_res(h, wts["w_down"], i, x, norm_gain=wts["final_norm"] if i == depth - 1 else None)
    return x.reshape(bsz, seq, d)


def kernel(x_prompt, x_sample, mem_prompt, mem_sample, norm_mix, a_wqkv, a_q_gain, a_k_gain, a_wo, b_wqkv, b_wo, norm_cross, norm_mem, c_wq, c_wkv, c_wo, norm_mlp, w_up, w_down, final_norm):
    a_wqkv, a_q_gain, a_k_gain = _permute_a_heads(a_wqkv, a_q_gain, a_k_gain)
    wts = dict(
        norm_mix=norm_mix, a_wqkv=a_wqkv.astype(BF16), a_q_gain=a_q_gain, a_k_gain=a_k_gain,
        a_wo=a_wo.astype(BF16), b_wqkv=b_wqkv.astype(BF16), b_wo=b_wo.astype(BF16),
        norm_cross=norm_cross, norm_mem=norm_mem, c_wq=c_wq.astype(BF16), c_wkv=c_wkv.astype(BF16),
        c_wo=c_wo.astype(BF16), norm_mlp=norm_mlp, w_up=w_up.astype(BF16), w_down=w_down.astype(BF16),
        final_norm=final_norm,
    )
    masks = jnp.asarray(_band_mask())
    outs = []
    for x, mem in ((x_prompt, mem_prompt), (x_sample, mem_sample)):
        tabs_a, tabs_b = _rope_tables(x.shape[1])
        outs.append(_trunk(x, mem, wts, tabs_a, tabs_b, masks))
    return tuple(outs)
```

```python
import functools

import numpy as np
import jax
import jax.numpy as jnp
from jax import lax
from jax.experimental import pallas as pl
from jax.experimental.pallas import tpu as pltpu

HEAD_DIM = 128
A_Q_HEADS = 16
A_KV_HEADS = 4
A_GROUP = A_Q_HEADS // A_KV_HEADS
B_GROUPS = ((128, 1), (512, 4), (2048, 16))
N_B_GROUPS = 3
B_HEADS = 8
C_HEADS = 4
GRID_W = 64
ROPE_THETA = 10000.0
EPS = 1e-6
ATTN_SCALE = HEAD_DIM ** -0.5
LOG2E = 1.4426950408889634
NEG_BIG = -1e30
BOUNDED_SCORE_LIMIT = 60.0

F32 = jnp.float32
BF16 = jnp.bfloat16

A_HEAD_PERM = np.concatenate([np.arange(0, 32), np.arange(64, 96), np.arange(32, 64), np.arange(96, 128)])

VMEM_LIMIT_BYTES = 56 * 1024 * 1024

MM_TM = 1024
MM_TN = 1024
MM_TK = 2048
QKV_B_TN = 1536
MLP_UP_TN = 2048
FLASH_TQ = 256
FLASH_TK = 1024
BAND_TILE = 2048
BAND_Q = 128
BAND_SIDE = 64
BAND_MIN_HALO = 256
CROSS_TM = 512


def _params(*sem):
    return pltpu.CompilerParams(dimension_semantics=sem, vmem_limit_bytes=VMEM_LIMIT_BYTES)


def _rms_rows(x, g):
    ms = jnp.mean(x * x, axis=-1, keepdims=True)
    return x * lax.rsqrt(ms + EPS) * g


def _layer_spec(block, layer, imap):
    return pl.BlockSpec((None,) + block, lambda *idx: (layer,) + imap(*idx))


def _mm_norm_kernel(x_ref, g_ref, w_ref, o_ref, xn_ref):
    @pl.when(pl.program_id(1) == 0)
    def _():
        xn_ref[...] = _rms_rows(x_ref[...], g_ref[...]).astype(BF16)

    o_ref[...] = jnp.dot(xn_ref[...], w_ref[...], preferred_element_type=F32).astype(o_ref.dtype)


def mm_norm(x, g, w, layer):
    m, k = x.shape
    n = w.shape[-1]
    tm = min(MM_TM, m)
    tn = min(MM_TN, n)
    return pl.pallas_call(
        _mm_norm_kernel,
        out_shape=jax.ShapeDtypeStruct((m, n), BF16),
        grid=(m // tm, n // tn),
        in_specs=[
            pl.BlockSpec((tm, k), lambda i, j: (i, 0)),
            pl.BlockSpec((1, k), lambda i, j: (0, 0)),
            _layer_spec((k, tn), layer, lambda i, j: (0, j)),
        ],
        out_specs=pl.BlockSpec((tm, tn), lambda i, j: (i, j)),
        scratch_shapes=[pltpu.VMEM((tm, k), BF16)],
        compiler_params=_params("parallel", "arbitrary"),
        name="mm_norm",
    )(x, g.reshape(1, k), w)


def _mm_relu2_kernel(a_ref, w_ref, o_ref):
    acc = jnp.dot(a_ref[...], w_ref[...], preferred_element_type=F32)
    o_ref[...] = jnp.square(jnp.maximum(acc, 0.0)).astype(o_ref.dtype)


def mm_relu2(a, w, layer):
    m, k = a.shape
    n = w.shape[-1]
    tm = min(MM_TM, m)
    tn = min(MLP_UP_TN, n)
    return pl.pallas_call(
        _mm_relu2_kernel,
        out_shape=jax.ShapeDtypeStruct((m, n), BF16),
        grid=(m // tm, n // tn),
        in_specs=[
            pl.BlockSpec((tm, k), lambda i, j: (i, 0)),
            _layer_spec((k, tn), layer, lambda i, j: (0, j)),
        ],
        out_specs=pl.BlockSpec((tm, tn), lambda i, j: (i, j)),
        compiler_params=_params("parallel", "arbitrary"),
        name="mm_relu2",
    )(a, w)


def _mm_res_kernel(a_ref, w_ref, r_ref, *rest, normed):
    o_ref = rest[-1]

    @pl.when(pl.program_id(2) == 0)
    def _():
        o_ref[...] = r_ref[...]

    o_ref[...] += jnp.dot(a_ref[...], w_ref[...], preferred_element_type=F32)

    if normed:
        @pl.when(pl.program_id(2) == pl.num_programs(2) - 1)
        def _():
            o_ref[...] = _rms_rows(o_ref[...], rest[0][...])


def mm_res(a, w, layer, res, norm_gain=None):
    m, k = a.shape
    n = w.shape[-1]
    normed = norm_gain is not None
    tm = min(MM_TM // 2, m) if normed else min(MM_TM, m)
    tn = n if normed else min(MM_TN, n)
    tk = min(MM_TK, k)
    in_specs = [
        pl.BlockSpec((tm, tk), lambda i, j, kk: (i, kk)),
        _layer_spec((tk, tn), layer, lambda i, j, kk: (kk, j)),
        pl.BlockSpec((tm, tn), lambda i, j, kk: (i, j)),
    ]
    args = [a, w, res]
    if normed:
        in_specs.append(pl.BlockSpec((1, n), lambda i, j, kk: (0, 0)))
        args.append(norm_gain.reshape(1, n))
    return pl.pallas_call(
        functools.partial(_mm_res_kernel, normed=normed),
        out_shape=jax.ShapeDtypeStruct((m, n), F32),
        grid=(m // tm, n // tn, k // tk),
        in_specs=in_specs,
        out_specs=pl.BlockSpec((tm, tn), lambda i, j, kk: (i, j)),
        compiler_params=_params("parallel", "parallel", "arbitrary"),
        name="mm_res_norm" if normed else "mm_res",
    )(*args)


def _rope_half(x, cos, sin_signed):
    return x * cos + pltpu.roll(x, HEAD_DIM // 2, 1) * sin_signed


def _qkv_a_kernel(x_ref, g_ref, w_ref, qg_ref, kg_ref, cos_ref, sin_ref, qk_ref, v_ref, xn_ref):
    j = pl.program_id(1)
    n_q_tiles = A_Q_HEADS // A_KV_HEADS

    @pl.when(j == 0)
    def _():
        xn_ref[...] = _rms_rows(x_ref[...], g_ref[...]).astype(BF16)

    acc = jnp.dot(xn_ref[...], w_ref[...], preferred_element_type=F32)

    @pl.when(j <= n_q_tiles)
    def _():
        gain = jnp.where(j < n_q_tiles, qg_ref[...] * (ATTN_SCALE * LOG2E), kg_ref[...])
        cos = cos_ref[...]
        sin = sin_ref[...]
        heads = []
        for h in range(A_KV_HEADS):
            sl = slice(h * HEAD_DIM, (h + 1) * HEAD_DIM)
            heads.append(_rope_half(_rms_rows(acc[:, sl], gain), cos, sin).astype(BF16))
        qk_ref[...] = jnp.concatenate(heads, axis=1)

    @pl.when(j == n_q_tiles + 1)
    def _():
        ones = jnp.ones((v_ref.shape[0], HEAD_DIM), BF16)
        for h in range(A_KV_HEADS):
            v_ref[:, 2 * h * HEAD_DIM:(2 * h + 1) * HEAD_DIM] = acc[:, h * HEAD_DIM:(h + 1) * HEAD_DIM].astype(BF16)
            v_ref[:, (2 * h + 1) * HEAD_DIM:(2 * h + 2) * HEAD_DIM] = ones


def qkv_a(x, g, w, layer, q_gain, k_gain, cos, sin_signed, seq):
    m, k = x.shape
    nkv = A_KV_HEADS * HEAD_DIM
    tm = MM_TM
    tn = nkv
    nt = seq // tm
    n_qk_tiles = (A_Q_HEADS + A_KV_HEADS) * HEAD_DIM // tn
    tab_spec = pl.BlockSpec((tm, HEAD_DIM), lambda i, j: (i % nt, 0))
    g_spec = pl.BlockSpec((1, HEAD_DIM), lambda i, j: (0, 0))
    return pl.pallas_call(
        _qkv_a_kernel,
        out_shape=(
            jax.ShapeDtypeStruct((m, n_qk_tiles * tn), BF16),
            jax.ShapeDtypeStruct((m, 2 * nkv), BF16),
        ),
        grid=(m // tm, w.shape[-1] // tn),
        in_specs=[
            pl.BlockSpec((tm, k), lambda i, j: (i, 0)),
            pl.BlockSpec((1, k), lambda i, j: (0, 0)),
            _layer_spec((k, tn), layer, lambda i, j: (0, j)),
            g_spec,
            g_spec,
            tab_spec,
            tab_spec,
        ],
        out_specs=(
            pl.BlockSpec((tm, tn), lambda i, j: (i, jnp.minimum(j, n_qk_tiles - 1))),
            pl.BlockSpec((tm, 2 * nkv), lambda i, j: (i, 0)),
        ),
        scratch_shapes=[pltpu.VMEM((tm, k), BF16)],
        compiler_params=_params("parallel", "arbitrary"),
        name="qkv_a",
    )(x, g.reshape(1, k), w, q_gain.reshape(1, HEAD_DIM), k_gain.reshape(1, HEAD_DIM), cos, sin_signed)


def _stack_q(q_ref, qs_ref, tq):
    for g in range(A_GROUP):
        qs_ref[g * tq:(g + 1) * tq, :] = q_ref[0, :, g * HEAD_DIM:(g + 1) * HEAD_DIM]


def _unstack_o(o, o_ref, tq):
    for g in range(A_GROUP):
        o_ref[0, :, g * HEAD_DIM:(g + 1) * HEAD_DIM] = o[g * tq:(g + 1) * tq, :].astype(o_ref.dtype)


def _flash_a_bounded_kernel(q_ref, k_ref, v_ref, o_ref, qs_ref, acc_ref, *, tq, tk, nk):
    _stack_q(q_ref, qs_ref, tq)
    acc_ref[...] = jnp.zeros(acc_ref.shape, F32)

    def step(c, carry):
        start = pl.multiple_of(c * tk, tk)
        kc = k_ref[0, pl.ds(start, tk), :]
        vc = v_ref[0, pl.ds(start, tk), :]
        s = lax.dot_general(qs_ref[...], kc, (((1,), (1,)), ((), ())), preferred_element_type=F32)
        acc_ref[...] += jnp.dot(jnp.exp2(s).astype(BF16), vc, preferred_element_type=F32)
        return carry

    lax.fori_loop(0, nk, step, 0, unroll=True)
    acc = acc_ref[...]
    _unstack_o(acc[:, :HEAD_DIM] / acc[:, HEAD_DIM:], o_ref, tq)


def _flash_a_online_kernel(q_ref, k_ref, v_ref, o_ref, qs_ref, m_ref, acc_ref, *, tq, tk, nk):
    _stack_q(q_ref, qs_ref, tq)
    m_ref[...] = jnp.full(m_ref.shape, NEG_BIG, F32)
    acc_ref[...] = jnp.zeros(acc_ref.shape, F32)

    def step(c, carry):
        start = pl.multiple_of(c * tk, tk)
        kc = k_ref[0, pl.ds(start, tk), :]
        vc = v_ref[0, pl.ds(start, tk), :]
        s = lax.dot_general(qs_ref[...], kc, (((1,), (1,)), ((), ())), preferred_element_type=F32)
        m_prev = m_ref[...]
        m_new = jnp.maximum(m_prev, jnp.max(s, axis=1, keepdims=True))
        alpha = jnp.exp2(m_prev - m_new)
        p = jnp.exp2(s - m_new)
        acc_ref[...] = alpha * acc_ref[...] + jnp.dot(p.astype(BF16), vc, preferred_element_type=F32)
        m_ref[...] = m_new
        return carry

    lax.fori_loop(0, nk, step, 0)
    acc = acc_ref[...]
    _unstack_o(acc[:, :HEAD_DIM] / acc[:, HEAD_DIM:], o_ref, tq)


def _flash_a_call(qk, v, *, bounded):
    bsz, seq, _ = qk.shape
    nq = A_Q_HEADS * HEAD_DIM
    tq, tk = FLASH_TQ, FLASH_TK
    gw = A_GROUP * HEAD_DIM
    rows = A_GROUP * tq
    scratch = [pltpu.VMEM((rows, HEAD_DIM), BF16)]
    if not bounded:
        scratch.append(pltpu.VMEM((rows, 1), F32))
    scratch.append(pltpu.VMEM((rows, 2 * HEAD_DIM), F32))
    body = _flash_a_bounded_kernel if bounded else _flash_a_online_kernel
    return pl.pallas_call(
        functools.partial(body, tq=tq, tk=tk, nk=seq // tk),
        out_shape=jax.ShapeDtypeStruct((bsz, seq, nq), BF16),
        grid=(bsz, A_KV_HEADS, seq // tq),
        in_specs=[
            pl.BlockSpec((1, tq, gw), lambda b, h, i: (b, i, h)),
            pl.BlockSpec((1, seq, HEAD_DIM), lambda b, h, i: (b, 0, A_Q_HEADS + h)),
            pl.BlockSpec((1, seq, 2 * HEAD_DIM), lambda b, h, i: (b, 0, h)),
        ],
        out_specs=pl.BlockSpec((1, tq, gw), lambda b, h, i: (b, i, h)),
        scratch_shapes=scratch,
        compiler_params=_params("parallel", "parallel", "arbitrary"),
        name="flash_a_bounded" if bounded else "flash_a_online",
    )(qk, qk, v)


def flash_a(qk, v, q_gain, k_gain):
    bound = HEAD_DIM * ATTN_SCALE * jnp.max(jnp.abs(q_gain)) * jnp.max(jnp.abs(k_gain))
    return lax.cond(
        bound <= BOUNDED_SCORE_LIMIT,
        functools.partial(_flash_a_call, bounded=True),
        functools.partial(_flash_a_call, bounded=False),
        qk, v,
    )


def _qkv_b_kernel(x_ref, g_ref, w_ref, cos_ref, sin_ref, o_ref, xn_ref, *, tiles_per_which):
    j = pl.program_id(1)

    @pl.when(j == 0)
    def _():
        xn_ref[...] = _rms_rows(x_ref[...], g_ref[...]).astype(BF16)

    acc = jnp.dot(xn_ref[...], w_ref[...], preferred_element_type=F32)

    rotary = j < 2 * tiles_per_which
    scale = jnp.where(j < tiles_per_which, ATTN_SCALE * LOG2E, 1.0).astype(F32)
    cos = jnp.where(rotary, cos_ref[...] * scale, 1.0)
    sin = jnp.where(rotary, sin_ref[...] * scale, 0.0)
    for h in range(o_ref.shape[1] // HEAD_DIM):
        sl = slice(h * HEAD_DIM, (h + 1) * HEAD_DIM)
        o_ref[:, sl] = _rope_half(acc[:, sl], cos, sin).astype(BF16)


def qkv_b(x, g, w, layer, cos, sin_signed, seq):
    m, k = x.shape
    n = w.shape[-1]
    tm = MM_TM
    tn = QKV_B_TN
    assert (n // 3) % tn == 0
    nt = seq // tm
    tab_spec = pl.BlockSpec((tm, HEAD_DIM), lambda i, j: (i % nt, 0))
    return pl.pallas_call(
        functools.partial(_qkv_b_kernel, tiles_per_which=n // 3 // tn),
        out_shape=jax.ShapeDtypeStruct((m, n), BF16),
        grid=(m // tm, n // tn),
        in_specs=[
            pl.BlockSpec((tm, k), lambda i, j: (i, 0)),
            pl.BlockSpec((1, k), lambda i, j: (0, 0)),
            _layer_spec((k, tn), layer, lambda i, j: (0, j)),
            tab_spec,
            tab_spec,
        ],
        out_specs=pl.BlockSpec((tm, tn), lambda i, j: (i, j)),
        scratch_shapes=[pltpu.VMEM((tm, k), BF16)],
        compiler_params=_params("parallel", "arbitrary"),
        name="qkv_b",
    )(x, g.reshape(1, k), w, cos, sin_signed)


def _band_mask():
    rel = np.arange(2 * BAND_Q)[None, :] - np.arange(BAND_Q)[:, None]
    return np.where((rel >= 0) & (rel <= BAND_Q), 0.0, NEG_BIG).astype(np.float32)


def _band_halo(dil):
    return max(BAND_SIDE * dil, BAND_MIN_HALO)


def _rows(start, size, stride):
    return pl.ds(start, size) if stride == 1 else pl.ds(start, size, stride=stride)


def _band_b_kernel(*refs):
    ng = N_B_GROUPS
    q_refs = refs[0:ng]
    k_refs = refs[ng:4 * ng]
    v_refs = refs[4 * ng:7 * ng]
    mask_ref = refs[7 * ng]
    o_ref = refs[7 * ng + 1]
    scratch = refs[7 * ng + 2:]
    dec = [g for g, (_, dil) in enumerate(B_GROUPS) if dil > 1]
    nd = len(dec)
    qf, kf, vf = (dict(zip(dec, scratch[n * nd:(n + 1) * nd])) for n in range(3))
    og, lg, mg = (scratch[3 * nd + n * ng:3 * nd + (n + 1) * ng] for n in range(3))
    tile = BAND_TILE
    i = pl.program_id(1)

    def window(pieces, lo, h):
        hi = lo + 2 * BAND_Q
        parts = []
        if lo < 0:
            parts.append(pieces[0][0, h + lo:h, :])
        parts.append(pieces[1][0, max(lo, 0):min(hi, tile), :])
        if hi > tile:
            parts.append(pieces[2][0, 0:hi - tile, :])
        return parts[0] if len(parts) == 1 else jnp.concatenate(parts, axis=0)

    for g in dec:
        dil = B_GROUPS[g][1]
        h = _band_halo(dil)
        qf[g][...] = q_refs[g][0].astype(F32)
        for src, dst in ((k_refs, kf[g]), (v_refs, vf[g])):
            dst[0:h, :] = src[3 * g][0].astype(F32)
            dst[h:h + tile, :] = src[3 * g + 1][0].astype(F32)
            dst[h + tile:, :] = src[3 * g + 2][0].astype(F32)

    col = lax.broadcasted_iota(jnp.int32, (1, 2 * BAND_Q), 1)
    lo_row = jnp.where((i == 0) & (col < BAND_SIDE), NEG_BIG, 0.0).astype(F32)
    hi_row = jnp.where((i == pl.num_programs(1) - 1) & (col >= 2 * BAND_Q - BAND_SIDE), NEG_BIG, 0.0).astype(F32)
    mask = mask_ref[...]
    masks = {(False, False): mask, (True, False): mask + lo_row,
             (False, True): mask + hi_row, (True, True): mask + lo_row + hi_row}
    ones = jnp.ones((2 * BAND_Q, HEAD_DIM), BF16)

    for g, (_, dil) in enumerate(B_GROUPS):
        h = _band_halo(dil)
        n_sub = tile // (BAND_Q * dil)
        for mb in range(n_sub):
            for r in range(dil):
                q_rows = _rows(dil * BAND_Q * mb + r, BAND_Q, dil)
                if dil == 1:
                    qb = q_refs[g][0, q_rows, :]
                    kb = window(k_refs[3 * g:3 * g + 3], BAND_Q * mb - BAND_SIDE, h)
                    vb = window(v_refs[3 * g:3 * g + 3], BAND_Q * mb - BAND_SIDE, h)
                else:
                    k_rows = _rows(h + dil * (BAND_Q * mb - BAND_SIDE) + r, 2 * BAND_Q, dil)
                    qb = qf[g][q_rows, :].astype(BF16)
                    kb = kf[g][k_rows, :].astype(BF16)
                    vb = vf[g][k_rows, :].astype(BF16)
                vb = jnp.concatenate([vb, ones], axis=1)
                s = lax.dot_general(qb, kb, (((1,), (1,)), ((), ())), preferred_element_type=F32)
                s = s + masks[(mb == 0, mb == n_sub - 1)]
                m = s.max(axis=1, keepdims=True)
                o = jnp.dot(jnp.exp2(s - m).astype(BF16), vb, preferred_element_type=F32)
                og[g][q_rows, :] = o[:, :HEAD_DIM]
                lg[g][q_rows, :] = o[:, HEAD_DIM:]
                mg[g][q_rows, :] = jnp.broadcast_to(m, (BAND_Q, HEAD_DIM))

    chunk = 2 * BAND_Q
    for c in range(tile // chunk):
        rows = slice(c * chunk, (c + 1) * chunk)
        ms = [mg[g][rows, :] for g in range(ng)]
        m = functools.reduce(jnp.maximum, ms)
        ws = [jnp.exp2(mi - m) for mi in ms]
        num = sum(ws[g] * og[g][rows, :] for g in range(ng))
        den = sum(ws[g] * lg[g][rows, :] for g in range(ng))
        o_ref[0, rows, :] = (num / den).astype(o_ref.dtype)


def band_b(qkv, mask):
    bsz, seq, _ = qkv.shape
    tile = BAND_TILE
    nh = N_B_GROUPS * B_HEADS
    assert all(win == 2 * BAND_SIDE * dil for win, dil in B_GROUPS)
    halos = [_band_halo(dil) for _, dil in B_GROUPS]
    assert all(tile % h == 0 and tile % (BAND_Q * dil) == 0 for h, (_, dil) in zip(halos, B_GROUPS))

    def col(which, g, h):
        return which * nh + g * B_HEADS + h

    def tile_spec(which, g):
        return pl.BlockSpec((1, tile, HEAD_DIM), lambda b, i, h: (b, i, col(which, g, h)))

    def halo_spec(which, g, side):
        per_tile = tile // halos[g]
        last = seq // halos[g] - 1

        def imap(b, i, h):
            blk = i * per_tile - 1 if side < 0 else (i + 1) * per_tile
            return (b, jnp.clip(blk, 0, last), col(which, g, h))
        return pl.BlockSpec((1, halos[g], HEAD_DIM), imap)

    def kv_specs(which):
        return [s for g in range(N_B_GROUPS)
                for s in (halo_spec(which, g, -1), tile_spec(which, g), halo_spec(which, g, +1))]

    in_specs = [tile_spec(0, g) for g in range(N_B_GROUPS)] + kv_specs(1) + kv_specs(2)
    in_specs += [pl.BlockSpec(mask.shape, lambda b, i, h: (0, 0))]
    n_in = len(in_specs) - 1
    dec_halos = [h for h, (_, dil) in zip(halos, B_GROUPS) if dil > 1]
    scratch = [pltpu.VMEM((tile, HEAD_DIM), F32) for _ in dec_halos]
    scratch += [pltpu.VMEM((tile + 2 * h, HEAD_DIM), F32) for _ in range(2) for h in dec_halos]
    scratch += [pltpu.VMEM((tile, HEAD_DIM), F32) for _ in range(3 * N_B_GROUPS)]
    return pl.pallas_call(
        _band_b_kernel,
        out_shape=jax.ShapeDtypeStruct((bsz, seq, B_HEADS * HEAD_DIM), BF16),
        grid=(bsz, seq // tile, B_HEADS),
        in_specs=in_specs,
        out_specs=pl.BlockSpec((1, tile, HEAD_DIM), lambda b, i, h: (b, i, h)),
        scratch_shapes=scratch,
        compiler_params=_params("parallel", "parallel", "arbitrary"),
        name="band_b",
    )(*([qkv] * n_in), mask)


def _cross_kernel(x_ref, g_ref, wq_ref, kv_ref, wo_ref, gn_ref, o_ref, on_ref):
    x = x_ref[...]
    xn = _rms_rows(x, g_ref[...]).astype(BF16)
    q = (jnp.dot(xn, wq_ref[...], preferred_element_type=F32) * ATTN_SCALE).astype(BF16)
    nkv = C_HEADS * HEAD_DIM
    outs = []
    for h in range(C_HEADS):
        sl = slice(h * HEAD_DIM, (h + 1) * HEAD_DIM)
        kh = kv_ref[0, :, sl]
        vh = kv_ref[0, :, nkv + h * HEAD_DIM:nkv + (h + 1) * HEAD_DIM]
        s = lax.dot_general(q[:, sl], kh, (((1,), (1,)), ((), ())), preferred_element_type=F32)
        p = jnp.exp(s - s.max(axis=1, keepdims=True))
        l = p.sum(axis=1, keepdims=True)
        outs.append((jnp.dot(p.astype(BF16), vh, preferred_element_type=F32) / l).astype(BF16))
    o = jnp.concatenate(outs, axis=1)
    y = x + jnp.dot(o, wo_ref[...], preferred_element_type=F32)
    o_ref[...] = y
    on_ref[...] = _rms_rows(y, gn_ref[...]).astype(BF16)


def cross_block(x, g, wq, kv, wo, layer, next_gain, seq):
    m, d = x.shape
    tm = CROSS_TM
    nt = seq // tm
    nkv = C_HEADS * HEAD_DIM
    mem_len = kv.shape[1]
    return pl.pallas_call(
        _cross_kernel,
        out_shape=(jax.ShapeDtypeStruct((m, d), F32), jax.ShapeDtypeStruct((m, d), BF16)),
        grid=(m // tm,),
        in_specs=[
            pl.BlockSpec((tm, d), lambda i: (i, 0)),
            pl.BlockSpec((1, d), lambda i: (0, 0)),
            _layer_spec((d, nkv), layer, lambda i: (0, 0)),
            pl.BlockSpec((1, mem_len, 2 * nkv), lambda i: (i // nt, 0, 0)),
            _layer_spec((nkv, d), layer, lambda i: (0, 0)),
            pl.BlockSpec((1, d), lambda i: (0, 0)),
        ],
        out_specs=(pl.BlockSpec((tm, d), lambda i: (i, 0)), pl.BlockSpec((tm, d), lambda i: (i, 0))),
        compiler_params=_params("parallel"),
        name="cross_block",
    )(x, g.reshape(1, d), wq, kv, wo, next_gain.reshape(1, d))


def _rope_angles(pos, dim):
    inv = ROPE_THETA ** (-jnp.arange(0, dim, 2, dtype=F32) / dim)
    ang = pos[:, None] * inv[None, :]
    return jnp.concatenate([ang, ang], axis=-1)


def _rope_tables(seq):
    rows = seq // GRID_W
    row = jnp.repeat(jnp.arange(rows, dtype=F32), GRID_W)
    col = jnp.tile(jnp.arange(GRID_W, dtype=F32), rows)
    half = HEAD_DIM // 2
    ang = jnp.concatenate([_rope_angles(row, half), _rope_angles(col, half)], axis=-1)
    low = jnp.arange(HEAD_DIM) < half
    cos_a = jnp.cos(ang)[:, A_HEAD_PERM]
    sin_a = jnp.sin(ang)[:, A_HEAD_PERM]
    sin_a = jnp.where(low, -sin_a, sin_a)
    a1 = _rope_angles(jnp.arange(seq, dtype=F32), HEAD_DIM)
    cos_b = jnp.cos(a1)
    sin_b = jnp.where(low, -jnp.sin(a1), jnp.sin(a1))
    return (cos_a, sin_a), (cos_b, sin_b)


def _permute_a_heads(a_wqkv, a_q_gain, a_k_gain):
    n_qk = (A_Q_HEADS + A_KV_HEADS) * HEAD_DIM
    cols = np.arange(a_wqkv.shape[-1])
    cols[:n_qk] = (cols[:n_qk].reshape(-1, HEAD_DIM)[:, A_HEAD_PERM]).reshape(-1)
    return a_wqkv[..., cols], a_q_gain[..., A_HEAD_PERM], a_k_gain[..., A_HEAD_PERM]


def _trunk(x, mem, wts, tabs_a, tabs_b, masks):
    bsz, seq, d = x.shape
    mem_len = mem.shape[1]
    x = x.reshape(bsz * seq, d)
    mem = mem.reshape(bsz * mem_len, d)
    depth = wts["norm_mix"].shape[0]
    for i in range(depth):
        j = i // 2
        if i % 2 == 0:
            qk, v = qkv_a(x, wts["norm_mix"][i], wts["a_wqkv"], j, wts["a_q_gain"][j], wts["a_k_gain"][j],
                          tabs_a[0], tabs_a[1], seq)
            o = flash_a(qk.reshape(bsz, seq, -1), v.reshape(bsz, seq, -1), wts["a_q_gain"][j], wts["a_k_gain"][j])
            x = mm_res(o.reshape(bsz * seq, -1), wts["a_wo"], j, x)
        else:
            qkv = qkv_b(x, wts["norm_mix"][i], wts["b_wqkv"], j, tabs_b[0], tabs_b[1], seq)
            o = band_b(qkv.reshape(bsz, seq, -1), masks)
            x = mm_res(o.reshape(bsz * seq, -1), wts["b_wo"], j, x)
        kv = mm_norm(mem, wts["norm_mem"][i], wts["c_wkv"], i)
        x, xn = cross_block(x, wts["norm_cross"][i], wts["c_wq"], kv.reshape(bsz, mem_len, -1), wts["c_wo"], i,
                            wts["norm_mlp"][i], seq)
        h = mm_relu2(xn, wts["w_up"], i)
        x = mm_res(h, wts["w_down"], i, x, norm_gain=wts["final_norm"] if i == depth - 1 else None)
    return x.reshape(bsz, seq, d)


def kernel(x_prompt, x_sample, mem_prompt, mem_sample, norm_mix, a_wqkv, a_q_gain, a_k_gain, a_wo, b_wqkv, b_wo, norm_cross, norm_mem, c_wq, c_wkv, c_wo, norm_mlp, w_up, w_down, final_norm):
    a_wqkv, a_q_gain, a_k_gain = _permute_a_heads(a_wqkv, a_q_gain, a_k_gain)
    wts = dict(
        norm_mix=norm_mix, a_wqkv=a_wqkv.astype(BF16), a_q_gain=a_q_gain, a_k_gain=a_k_gain,
        a_wo=a_wo.astype(BF16), b_wqkv=b_wqkv.astype(BF16), b_wo=b_wo.astype(BF16),
        norm_cross=norm_cross, norm_mem=norm_mem, c_wq=c_wq.astype(BF16), c_wkv=c_wkv.astype(BF16),
        c_wo=c_wo.astype(BF16), norm_mlp=norm_mlp, w_up=w_up.astype(BF16), w_down=w_down.astype(BF16),
        final_norm=final_norm,
    )
    masks = jnp.asarray(_band_mask())
    outs = []
    for x, mem in ((x_prompt, mem_prompt), (x_sample, mem_sample)):
        tabs_a, tabs_b = _rope_tables(x.shape[1])
        outs.append(_trunk(x, mem, wts, tabs_a, tabs_b, masks))
    return tuple(outs)
```

```python
import functools

import numpy as np
import jax
import jax.numpy as jnp
from jax import lax
from jax.experimental import pallas as pl
from jax.experimental.pallas import tpu as pltpu

HEAD_DIM = 128
A_Q_HEADS = 16
A_KV_HEADS = 4
A_GROUP = A_Q_HEADS // A_KV_HEADS
B_GROUPS = ((128, 1), (512, 4), (2048, 16))
N_B_GROUPS = 3
B_HEADS = 8
C_HEADS = 4
GRID_W = 64
ROPE_THETA = 10000.0
EPS = 1e-6
ATTN_SCALE = HEAD_DIM ** -0.5
LOG2E = 1.4426950408889634
NEG_BIG = -1e30
BOUNDED_SCORE_LIMIT = 60.0

F32 = jnp.float32
BF16 = jnp.bfloat16

A_HEAD_PERM = np.concatenate([np.arange(0, 32), np.arange(64, 96), np.arange(32, 64), np.arange(96, 128)])

VMEM_LIMIT_BYTES = 56 * 1024 * 1024

MM_TM = 1024
MM_TN = 1024
MM_TK = 2048
QKV_B_TN = 1536
MLP_UP_TN = 2048
FLASH_TQ = 256
FLASH_TK = 1024
BAND_TILE = 2048
BAND_Q = 128
BAND_SIDE = 64
BAND_MIN_HALO = 256
CROSS_TM = 1024
CROSS_CHAINS = 2


def _params(*sem):
    return pltpu.CompilerParams(dimension_semantics=sem, vmem_limit_bytes=VMEM_LIMIT_BYTES)


def _rms_rows(x, g):
    ms = jnp.mean(x * x, axis=-1, keepdims=True)
    return x * lax.rsqrt(ms + EPS) * g


def _layer_spec(block, layer, imap):
    return pl.BlockSpec((None,) + block, lambda *idx: (layer,) + imap(*idx))


def _mm_norm_kernel(x_ref, g_ref, w_ref, o_ref, xn_ref):
    @pl.when(pl.program_id(1) == 0)
    def _():
        xn_ref[...] = _rms_rows(x_ref[...], g_ref[...]).astype(BF16)

    o_ref[...] = jnp.dot(xn_ref[...], w_ref[...], preferred_element_type=F32).astype(o_ref.dtype)


def mm_norm(x, g, w, layer):
    m, k = x.shape
    n = w.shape[-1]
    tm = min(MM_TM, m)
    tn = min(MM_TN, n)
    return pl.pallas_call(
        _mm_norm_kernel,
        out_shape=jax.ShapeDtypeStruct((m, n), BF16),
        grid=(m // tm, n // tn),
        in_specs=[
            pl.BlockSpec((tm, k), lambda i, j: (i, 0)),
            pl.BlockSpec((1, k), lambda i, j: (0, 0)),
            _layer_spec((k, tn), layer, lambda i, j: (0, j)),
        ],
        out_specs=pl.BlockSpec((tm, tn), lambda i, j: (i, j)),
        scratch_shapes=[pltpu.VMEM((tm, k), BF16)],
        compiler_params=_params("parallel", "arbitrary"),
        name="mm_norm",
    )(x, g.reshape(1, k), w)


def _mm_relu2_kernel(a_ref, w_ref, o_ref):
    acc = jnp.dot(a_ref[...], w_ref[...], preferred_element_type=F32)
    o_ref[...] = jnp.square(jnp.maximum(acc, 0.0)).astype(o_ref.dtype)


def mm_relu2(a, w, layer):
    m, k = a.shape
    n = w.shape[-1]
    tm = min(MM_TM, m)
    tn = min(MLP_UP_TN, n)
    return pl.pallas_call(
        _mm_relu2_kernel,
        out_shape=jax.ShapeDtypeStruct((m, n), BF16),
        grid=(m // tm, n // tn),
        in_specs=[
            pl.BlockSpec((tm, k), lambda i, j: (i, 0)),
            _layer_spec((k, tn), layer, lambda i, j: (0, j)),
        ],
        out_specs=pl.BlockSpec((tm, tn), lambda i, j: (i, j)),
        compiler_params=_params("parallel", "arbitrary"),
        name="mm_relu2",
    )(a, w)


def _mm_res_kernel(a_ref, w_ref, r_ref, *rest, normed, single_k):
    o_ref = rest[-1]

    if single_k:
        o_ref[...] = r_ref[...] + jnp.dot(a_ref[...], w_ref[...], preferred_element_type=F32)
    else:
        @pl.when(pl.program_id(2) == 0)
        def _():
            o_ref[...] = r_ref[...]

        o_ref[...] += jnp.dot(a_ref[...], w_ref[...], preferred_element_type=F32)

    if normed:
        @pl.when(pl.program_id(2) == pl.num_programs(2) - 1)
        def _():
            o_ref[...] = _rms_rows(o_ref[...], rest[0][...])


def mm_res(a, w, layer, res, norm_gain=None):
    m, k = a.shape
    n = w.shape[-1]
    normed = norm_gain is not None
    tm = min(MM_TM // 2, m) if normed else min(MM_TM, m)
    tn = n if normed else min(MM_TN, n)
    tk = min(MM_TK, k)
    in_specs = [
        pl.BlockSpec((tm, tk), lambda i, j, kk: (i, kk)),
        _layer_spec((tk, tn), layer, lambda i, j, kk: (kk, j)),
        pl.BlockSpec((tm, tn), lambda i, j, kk: (i, j)),
    ]
    args = [a, w, res]
    if normed:
        in_specs.append(pl.BlockSpec((1, n), lambda i, j, kk: (0, 0)))
        args.append(norm_gain.reshape(1, n))
    return pl.pallas_call(
        functools.partial(_mm_res_kernel, normed=normed, single_k=(k == tk)),
        out_shape=jax.ShapeDtypeStruct((m, n), F32),
        grid=(m // tm, n // tn, k // tk),
        in_specs=in_specs,
        out_specs=pl.BlockSpec((tm, tn), lambda i, j, kk: (i, j)),
        compiler_params=_params("parallel", "parallel", "arbitrary"),
        name="mm_res_norm" if normed else "mm_res",
    )(*args)


def _rope_half(x, cos, sin_signed):
    return x * cos + pltpu.roll(x, HEAD_DIM // 2, 1) * sin_signed


def _qkv_a_kernel(x_ref, g_ref, w_ref, qg_ref, kg_ref, cos_ref, sin_ref, qk_ref, v_ref, xn_ref):
    j = pl.program_id(1)
    n_q_tiles = A_Q_HEADS // A_KV_HEADS

    @pl.when(j == 0)
    def _():
        xn_ref[...] = _rms_rows(x_ref[...], g_ref[...]).astype(BF16)

    acc = jnp.dot(xn_ref[...], w_ref[...], preferred_element_type=F32)

    @pl.when(j <= n_q_tiles)
    def _():
        gain = jnp.where(j < n_q_tiles, qg_ref[...] * (ATTN_SCALE * LOG2E), kg_ref[...])
        cos = cos_ref[...]
        sin = sin_ref[...]
        heads = []
        for h in range(A_KV_HEADS):
            sl = slice(h * HEAD_DIM, (h + 1) * HEAD_DIM)
            heads.append(_rope_half(_rms_rows(acc[:, sl], gain), cos, sin).astype(BF16))
        qk_ref[...] = jnp.concatenate(heads, axis=1)

    @pl.when(j == n_q_tiles + 1)
    def _():
        ones = jnp.ones((v_ref.shape[0], HEAD_DIM), BF16)
        for h in range(A_KV_HEADS):
            v_ref[:, 2 * h * HEAD_DIM:(2 * h + 1) * HEAD_DIM] = acc[:, h * HEAD_DIM:(h + 1) * HEAD_DIM].astype(BF16)
            v_ref[:, (2 * h + 1) * HEAD_DIM:(2 * h + 2) * HEAD_DIM] = ones


def qkv_a(x, g, w, layer, q_gain, k_gain, cos, sin_signed, seq):
    m, k = x.shape
    nkv = A_KV_HEADS * HEAD_DIM
    tm = MM_TM
    tn = nkv
    nt = seq // tm
    n_qk_tiles = (A_Q_HEADS + A_KV_HEADS) * HEAD_DIM // tn
    tab_spec = pl.BlockSpec((tm, HEAD_DIM), lambda i, j: (i % nt, 0))
    g_spec = pl.BlockSpec((1, HEAD_DIM), lambda i, j: (0, 0))
    return pl.pallas_call(
        _qkv_a_kernel,
        out_shape=(
            jax.ShapeDtypeStruct((m, n_qk_tiles * tn), BF16),
            jax.ShapeDtypeStruct((m, 2 * nkv), BF16),
        ),
        grid=(m // tm, w.shape[-1] // tn),
        in_specs=[
            pl.BlockSpec((tm, k), lambda i, j: (i, 0)),
            pl.BlockSpec((1, k), lambda i, j: (0, 0)),
            _layer_spec((k, tn), layer, lambda i, j: (0, j)),
            g_spec,
            g_spec,
            tab_spec,
            tab_spec,
        ],
        out_specs=(
            pl.BlockSpec((tm, tn), lambda i, j: (i, jnp.minimum(j, n_qk_tiles - 1))),
            pl.BlockSpec((tm, 2 * nkv), lambda i, j: (i, 0)),
        ),
        scratch_shapes=[pltpu.VMEM((tm, k), BF16)],
        compiler_params=_params("parallel", "arbitrary"),
        name="qkv_a",
    )(x, g.reshape(1, k), w, q_gain.reshape(1, HEAD_DIM), k_gain.reshape(1, HEAD_DIM), cos, sin_signed)


def _stack_q(q_ref, qs_ref, tq):
    for g in range(A_GROUP):
        qs_ref[g * tq:(g + 1) * tq, :] = q_ref[0, :, g * HEAD_DIM:(g + 1) * HEAD_DIM]


def _unstack_o(o, o_ref, tq):
    for g in range(A_GROUP):
        o_ref[0, :, g * HEAD_DIM:(g + 1) * HEAD_DIM] = o[g * tq:(g + 1) * tq, :].astype(o_ref.dtype)


def _flash_a_bounded_kernel(q_ref, k_ref, v_ref, o_ref, qs_ref, acc_ref, *, tq, tk, nk):
    _stack_q(q_ref, qs_ref, tq)
    acc_ref[...] = jnp.zeros(acc_ref.shape, F32)

    def step(c, carry):
        start = pl.multiple_of(c * tk, tk)
        kc = k_ref[0, pl.ds(start, tk), :]
        vc = v_ref[0, pl.ds(start, tk), :]
        s = lax.dot_general(qs_ref[...], kc, (((1,), (1,)), ((), ())), preferred_element_type=F32)
        acc_ref[...] += jnp.dot(jnp.exp2(s).astype(BF16), vc, preferred_element_type=F32)
        return carry

    lax.fori_loop(0, nk, step, 0, unroll=True)
    acc = acc_ref[...]
    _unstack_o(acc[:, :HEAD_DIM] / acc[:, HEAD_DIM:], o_ref, tq)


def _flash_a_online_kernel(q_ref, k_ref, v_ref, o_ref, qs_ref, m_ref, acc_ref, *, tq, tk, nk):
    _stack_q(q_ref, qs_ref, tq)
    m_ref[...] = jnp.full(m_ref.shape, NEG_BIG, F32)
    acc_ref[...] = jnp.zeros(acc_ref.shape, F32)

    def step(c, carry):
        start = pl.multiple_of(c * tk, tk)
        kc = k_ref[0, pl.ds(start, tk), :]
        vc = v_ref[0, pl.ds(start, tk), :]
        s = lax.dot_general(qs_ref[...], kc, (((1,), (1,)), ((), ())), preferred_element_type=F32)
        m_prev = m_ref[...]
        m_new = jnp.maximum(m_prev, jnp.max(s, axis=1, keepdims=True))
        alpha = jnp.exp2(m_prev - m_new)
        p = jnp.exp2(s - m_new)
        acc_ref[...] = alpha * acc_ref[...] + jnp.dot(p.astype(BF16), vc, preferred_element_type=F32)
        m_ref[...] = m_new
        return carry

    lax.fori_loop(0, nk, step, 0)
    acc = acc_ref[...]
    _unstack_o(acc[:, :HEAD_DIM] / acc[:, HEAD_DIM:], o_ref, tq)


def _flash_a_call(qk, v, *, bounded):
    bsz, seq, _ = qk.shape
    nq = A_Q_HEADS * HEAD_DIM
    tq, tk = FLASH_TQ, FLASH_TK
    gw = A_GROUP * HEAD_DIM
    rows = A_GROUP * tq
    scratch = [pltpu.VMEM((rows, HEAD_DIM), BF16)]
    if not bounded:
        scratch.append(pltpu.VMEM((rows, 1), F32))
    scratch.append(pltpu.VMEM((rows, 2 * HEAD_DIM), F32))
    body = _flash_a_bounded_kernel if bounded else _flash_a_online_kernel
    return pl.pallas_call(
        functools.partial(body, tq=tq, tk=tk, nk=seq // tk),
        out_shape=jax.ShapeDtypeStruct((bsz, seq, nq), BF16),
        grid=(bsz, A_KV_HEADS, seq // tq),
        in_specs=[
            pl.BlockSpec((1, tq, gw), lambda b, h, i: (b, i, h)),
            pl.BlockSpec((1, seq, HEAD_DIM), lambda b, h, i: (b, 0, A_Q_HEADS + h)),
            pl.BlockSpec((1, seq, 2 * HEAD_DIM), lambda b, h, i: (b, 0, h)),
        ],
        out_specs=pl.BlockSpec((1, tq, gw), lambda b, h, i: (b, i, h)),
        scratch_shapes=scratch,
        compiler_params=_params("parallel", "parallel", "arbitrary"),
        name="flash_a_bounded" if bounded else "flash_a_online",
    )(qk, qk, v)


def flash_a(qk, v, q_gain, k_gain):
    bound = HEAD_DIM * ATTN_SCALE * jnp.max(jnp.abs(q_gain)) * jnp.max(jnp.abs(k_gain))
    return lax.cond(
        bound <= BOUNDED_SCORE_LIMIT,
        functools.partial(_flash_a_call, bounded=True),
        functools.partial(_flash_a_call, bounded=False),
        qk, v,
    )


def _qkv_b_kernel(x_ref, g_ref, w_ref, cos_ref, sin_ref, o_ref, xn_ref, *, tiles_per_which):
    j = pl.program_id(1)

    @pl.when(j == 0)
    def _():
        xn_ref[...] = _rms_rows(x_ref[...], g_ref[...]).astype(BF16)

    acc = jnp.dot(xn_ref[...], w_ref[...], preferred_element_type=F32)

    rotary = j < 2 * tiles_per_which
    scale = jnp.where(j < tiles_per_which, ATTN_SCALE * LOG2E, 1.0).astype(F32)
    cos = jnp.where(rotary, cos_ref[...] * scale, 1.0)
    sin = jnp.where(rotary, sin_ref[...] * scale, 0.0)
    for h in range(o_ref.shape[1] // HEAD_DIM):
        sl = slice(h * HEAD_DIM, (h + 1) * HEAD_DIM)
        o_ref[:, sl] = _rope_half(acc[:, sl], cos, sin).astype(BF16)


def qkv_b(x, g, w, layer, cos, sin_signed, seq):
    m, k = x.shape
    n = w.shape[-1]
    tm = MM_TM
    tn = QKV_B_TN
    assert (n // 3) % tn == 0
    nt = seq // tm
    tab_spec = pl.BlockSpec((tm, HEAD_DIM), lambda i, j: (i % nt, 0))
    return pl.pallas_call(
        functools.partial(_qkv_b_kernel, tiles_per_which=n // 3 // tn),
        out_shape=jax.ShapeDtypeStruct((m, n), BF16),
        grid=(m // tm, n // tn),
        in_specs=[
            pl.BlockSpec((tm, k), lambda i, j: (i, 0)),
            pl.BlockSpec((1, k), lambda i, j: (0, 0)),
            _layer_spec((k, tn), layer, lambda i, j: (0, j)),
            tab_spec,
            tab_spec,
        ],
        out_specs=pl.BlockSpec((tm, tn), lambda i, j: (i, j)),
        scratch_shapes=[pltpu.VMEM((tm, k), BF16)],
        compiler_params=_params("parallel", "arbitrary"),
        name="qkv_b",
    )(x, g.reshape(1, k), w, cos, sin_signed)


def _band_mask():
    rel = np.arange(2 * BAND_Q)[None, :] - np.arange(BAND_Q)[:, None]
    return np.where((rel >= 0) & (rel <= BAND_Q), 0.0, NEG_BIG).astype(np.float32)


def _band_halo(dil):
    return max(BAND_SIDE * dil, BAND_MIN_HALO)


def _rows(start, size, stride):
    return pl.ds(start, size) if stride == 1 else pl.ds(start, size, stride=stride)


def _band_b_kernel(*refs):
    ng = N_B_GROUPS
    q_refs = refs[0:ng]
    k_refs = refs[ng:4 * ng]
    v_refs = refs[4 * ng:7 * ng]
    mask_ref = refs[7 * ng]
    o_ref = refs[7 * ng + 1]
    scratch = refs[7 * ng + 2:]
    dec = [g for g, (_, dil) in enumerate(B_GROUPS) if dil > 1]
    nd = len(dec)
    qf, kf, vf = (dict(zip(dec, scratch[n * nd:(n + 1) * nd])) for n in range(3))
    og, lg, mg = (scratch[3 * nd + n * ng:3 * nd + (n + 1) * ng] for n in range(3))
    tile = BAND_TILE
    i = pl.program_id(1)

    def window(pieces, lo, h):
        hi = lo + 2 * BAND_Q
        parts = []
        if lo < 0:
            parts.append(pieces[0][0, h + lo:h, :])
        parts.append(pieces[1][0, max(lo, 0):min(hi, tile), :])
        if hi > tile:
            parts.append(pieces[2][0, 0:hi - tile, :])
        return parts[0] if len(parts) == 1 else jnp.concatenate(parts, axis=0)

    for g in dec:
        dil = B_GROUPS[g][1]
        h = _band_halo(dil)
        qf[g][...] = q_refs[g][0].astype(F32)
        for src, dst in ((k_refs, kf[g]), (v_refs, vf[g])):
            dst[0:h, :] = src[3 * g][0].astype(F32)
            dst[h:h + tile, :] = src[3 * g + 1][0].astype(F32)
            dst[h + tile:, :] = src[3 * g + 2][0].astype(F32)

    col = lax.broadcasted_iota(jnp.int32, (1, 2 * BAND_Q), 1)
    lo_row = jnp.where((i == 0) & (col < BAND_SIDE), NEG_BIG, 0.0).astype(F32)
    hi_row = jnp.where((i == pl.num_programs(1) - 1) & (col >= 2 * BAND_Q - BAND_SIDE), NEG_BIG, 0.0).astype(F32)
    mask = mask_ref[...]
    masks = {(False, False): mask, (True, False): mask + lo_row,
             (False, True): mask + hi_row, (True, True): mask + lo_row + hi_row}
    ones = jnp.ones((2 * BAND_Q, HEAD_DIM), BF16)

    for g, (_, dil) in enumerate(B_GROUPS):
        h = _band_halo(dil)
        n_sub = tile // (BAND_Q * dil)
        for mb in range(n_sub):
            for r in range(dil):
                q_rows = _rows(dil * BAND_Q * mb + r, BAND_Q, dil)
                if dil == 1:
                    qb = q_refs[g][0, q_rows, :]
                    kb = window(k_refs[3 * g:3 * g + 3], BAND_Q * mb - BAND_SIDE, h)
                    vb = window(v_refs[3 * g:3 * g + 3], BAND_Q * mb - BAND_SIDE, h)
                else:
                    k_rows = _rows(h + dil * (BAND_Q * mb - BAND_SIDE) + r, 2 * BAND_Q, dil)
                    qb = qf[g][q_rows, :].astype(BF16)
                    kb = kf[g][k_rows, :].astype(BF16)
                    vb = vf[g][k_rows, :].astype(BF16)
                vb = jnp.concatenate([vb, ones], axis=1)
                s = lax.dot_general(qb, kb, (((1,), (1,)), ((), ())), preferred_element_type=F32)
                s = s + masks[(mb == 0, mb == n_sub - 1)]
                m = s.max(axis=1, keepdims=True)
                o = jnp.dot(jnp.exp2(s - m).astype(BF16), vb, preferred_element_type=F32)
                og[g][q_rows, :] = o[:, :HEAD_DIM]
                lg[g][q_rows, :] = o[:, HEAD_DIM:]
                mg[g][q_rows, :] = jnp.broadcast_to(m, (BAND_Q, HEAD_DIM))

    chunk = 2 * BAND_Q
    for c in range(tile // chunk):
        rows = slice(c * chunk, (c + 1) * chunk)
        ms = [mg[g][rows, :] for g in range(ng)]
        m = functools.reduce(jnp.maximum, ms)
        ws = [jnp.exp2(mi - m) for mi in ms]
        num = sum(ws[g] * og[g][rows, :] for g in range(ng))
        den = sum(ws[g] * lg[g][rows, :] for g in range(ng))
        o_ref[0, rows, :] = (num / den).astype(o_ref.dtype)


def band_b(qkv, mask):
    bsz, seq, _ = qkv.shape
    tile = BAND_TILE
    nh = N_B_GROUPS * B_HEADS
    assert all(win == 2 * BAND_SIDE * dil for win, dil in B_GROUPS)
    halos = [_band_halo(dil) for _, dil in B_GROUPS]
    assert all(tile % h == 0 and tile % (BAND_Q * dil) == 0 for h, (_, dil) in zip(halos, B_GROUPS))

    def col(which, g, h):
        return which * nh + g * B_HEADS + h

    def tile_spec(which, g):
        return pl.BlockSpec((1, tile, HEAD_DIM), lambda b, i, h: (b, i, col(which, g, h)))

    def halo_spec(which, g, side):
        per_tile = tile // halos[g]
        last = seq // halos[g] - 1

        def imap(b, i, h):
            blk = i * per_tile - 1 if side < 0 else (i + 1) * per_tile
            return (b, jnp.clip(blk, 0, last), col(which, g, h))
        return pl.BlockSpec((1, halos[g], HEAD_DIM), imap)

    def kv_specs(which):
        return [s for g in range(N_B_GROUPS)
                for s in (halo_spec(which, g, -1), tile_spec(which, g), halo_spec(which, g, +1))]

    in_specs = [tile_spec(0, g) for g in range(N_B_GROUPS)] + kv_specs(1) + kv_specs(2)
    in_specs += [pl.BlockSpec(mask.shape, lambda b, i, h: (0, 0))]
    n_in = len(in_specs) - 1
    dec_halos = [h for h, (_, dil) in zip(halos, B_GROUPS) if dil > 1]
    scratch = [pltpu.VMEM((tile, HEAD_DIM), F32) for _ in dec_halos]
    scratch += [pltpu.VMEM((tile + 2 * h, HEAD_DIM), F32) for _ in range(2) for h in dec_halos]
    scratch += [pltpu.VMEM((tile, HEAD_DIM), F32) for _ in range(3 * N_B_GROUPS)]
    return pl.pallas_call(
        _band_b_kernel,
        out_shape=jax.ShapeDtypeStruct((bsz, seq, B_HEADS * HEAD_DIM), BF16),
        grid=(bsz, seq // tile, B_HEADS),
        in_specs=in_specs,
        out_specs=pl.BlockSpec((1, tile, HEAD_DIM), lambda b, i, h: (b, i, h)),
        scratch_shapes=scratch,
        compiler_params=_params("parallel", "parallel", "arbitrary"),
        name="band_b",
    )(*([qkv] * n_in), mask)


def _cross_kernel(x_ref, g_ref, wq_ref, kv_ref, wo_ref, gn_ref, o_ref, on_ref):
    nkv = C_HEADS * HEAD_DIM
    chain = x_ref.shape[0] // CROSS_CHAINS
    for c in range(CROSS_CHAINS):
        rows = slice(c * chain, (c + 1) * chain)
        x = x_ref[rows, :]
        xn = _rms_rows(x, g_ref[...]).astype(BF16)
        q = (jnp.dot(xn, wq_ref[...], preferred_element_type=F32) * ATTN_SCALE).astype(BF16)
        outs = []
        for h in range(C_HEADS):
            sl = slice(h * HEAD_DIM, (h + 1) * HEAD_DIM)
            kh = kv_ref[0, :, sl]
            vh = kv_ref[0, :, nkv + h * HEAD_DIM:nkv + (h + 1) * HEAD_DIM]
            s = lax.dot_general(q[:, sl], kh, (((1,), (1,)), ((), ())), preferred_element_type=F32)
            p = jnp.exp(s - s.max(axis=1, keepdims=True))
            l = p.sum(axis=1, keepdims=True)
            outs.append((jnp.dot(p.astype(BF16), vh, preferred_element_type=F32) / l).astype(BF16))
        o = jnp.concatenate(outs, axis=1)
        y = x + jnp.dot(o, wo_ref[...], preferred_element_type=F32)
        o_ref[rows, :] = y
        on_ref[rows, :] = _rms_rows(y, gn_ref[...]).astype(BF16)


def cross_block(x, g, wq, kv, wo, layer, next_gain, seq):
    m, d = x.shape
    tm = CROSS_TM
    nt = seq // tm
    nkv = C_HEADS * HEAD_DIM
    mem_len = kv.shape[1]
    return pl.pallas_call(
        _cross_kernel,
        out_shape=(jax.ShapeDtypeStruct((m, d), F32), jax.ShapeDtypeStruct((m, d), BF16)),
        grid=(m // tm,),
        in_specs=[
            pl.BlockSpec((tm, d), lambda i: (i, 0)),
            pl.BlockSpec((1, d), lambda i: (0, 0)),
            _layer_spec((d, nkv), layer, lambda i: (0, 0)),
            pl.BlockSpec((1, mem_len, 2 * nkv), lambda i: (i // nt, 0, 0)),
            _layer_spec((nkv, d), layer, lambda i: (0, 0)),
            pl.BlockSpec((1, d), lambda i: (0, 0)),
        ],
        out_specs=(pl.BlockSpec((tm, d), lambda i: (i, 0)), pl.BlockSpec((tm, d), lambda i: (i, 0))),
        compiler_params=_params("parallel"),
        name="cross_block",
    )(x, g.reshape(1, d), wq, kv, wo, next_gain.reshape(1, d))


def _rope_angles(pos, dim):
    inv = ROPE_THETA ** (-jnp.arange(0, dim, 2, dtype=F32) / dim)
    ang = pos[:, None] * inv[None, :]
    return jnp.concatenate([ang, ang], axis=-1)


def _rope_tables(seq):
    rows = seq // GRID_W
    row = jnp.repeat(jnp.arange(rows, dtype=F32), GRID_W)
    col = jnp.tile(jnp.arange(GRID_W, dtype=F32), rows)
    half = HEAD_DIM // 2
    ang = jnp.concatenate([_rope_angles(row, half), _rope_angles(col, half)], axis=-1)
    low = jnp.arange(HEAD_DIM) < half
    cos_a = jnp.cos(ang)[:, A_HEAD_PERM]
    sin_a = jnp.sin(ang)[:, A_HEAD_PERM]
    sin_a = jnp.where(low, -sin_a, sin_a)
    a1 = _rope_angles(jnp.arange(seq, dtype=F32), HEAD_DIM)
    cos_b = jnp.cos(a1)
    sin_b = jnp.where(low, -jnp.sin(a1), jnp.sin(a1))
    return (cos_a, sin_a), (cos_b, sin_b)


def _permute_a_heads(a_wqkv, a_q_gain, a_k_gain):
    n_qk = (A_Q_HEADS + A_KV_HEADS) * HEAD_DIM
    cols = np.arange(a_wqkv.shape[-1])
    cols[:n_qk] = (cols[:n_qk].reshape(-1, HEAD_DIM)[:, A_HEAD_PERM]).reshape(-1)
    return a_wqkv[..., cols], a_q_gain[..., A_HEAD_PERM], a_k_gain[..., A_HEAD_PERM]


def _trunk(x, mem, wts, tabs_a, tabs_b, masks):
    bsz, seq, d = x.shape
    mem_len = mem.shape[1]
    x = x.reshape(bsz * seq, d)
    mem = mem.reshape(bsz * mem_len, d)
    depth = wts["norm_mix"].shape[0]
    for i in range(depth):
        j = i // 2
        if i % 2 == 0:
            qk, v = qkv_a(x, wts["norm_mix"][i], wts["a_wqkv"], j, wts["a_q_gain"][j], wts["a_k_gain"][j],
                          tabs_a[0], tabs_a[1], seq)
            o = flash_a(qk.reshape(bsz, seq, -1), v.reshape(bsz, seq, -1), wts["a_q_gain"][j], wts["a_k_gain"][j])
            x = mm_res(o.reshape(bsz * seq, -1), wts["a_wo"], j, x)
        else:
            qkv = qkv_b(x, wts["norm_mix"][i], wts["b_wqkv"], j, tabs_b[0], tabs_b[1], seq)
            o = band_b(qkv.reshape(bsz, seq, -1), masks)
            x = mm_res(o.reshape(bsz * seq, -1), wts["b_wo"], j, x)
        kv = mm_norm(mem, wts["norm_mem"][i], wts["c_wkv"], i)
        x, xn = cross_block(x, wts["norm_cross"][i], wts["c_wq"], kv.reshape(bsz, mem_len, -1), wts["c_wo"], i,
                            wts["norm_mlp"][i], seq)
        h = mm_relu2(xn, wts["w_up"], i)
        x = mm_res(h, wts["w_down"], i, x, norm_gain=wts["final_norm"] if i == depth - 1 else None)
    return x.reshape(bsz, seq, d)


def kernel(x_prompt, x_sample, mem_prompt, mem_sample, norm_mix, a_wqkv, a_q_gain, a_k_gain, a_wo, b_wqkv, b_wo, norm_cross, norm_mem, c_wq, c_wkv, c_wo, norm_mlp, w_up, w_down, final_norm):
    a_wqkv, a_q_gain, a_k_gain = _permute_a_heads(a_wqkv, a_q_gain, a_k_gain)
    wts = dict(
        norm_mix=norm_mix, a_wqkv=a_wqkv.astype(BF16), a_q_gain=a_q_gain, a_k_gain=a_k_gain,
        a_wo=a_wo.astype(BF16), b_wqkv=b_wqkv.astype(BF16), b_wo=b_wo.astype(BF16),
        norm_cross=norm_cross, norm_mem=norm_mem, c_wq=c_wq.astype(BF16), c_wkv=c_wkv.astype(BF16),
        c_wo=c_wo.astype(BF16), norm_mlp=norm_mlp, w_up=w_up.astype(BF16), w_down=w_down.astype(BF16),
        final_norm=final_norm,
    )
    masks = jnp.asarray(_band_mask())
    outs = []
    for x, mem in ((x_prompt, mem_prompt), (x_sample, mem_sample)):
        tabs_a, tabs_b = _rope_tables(x.shape[1])
        outs.append(_trunk(x, mem, wts, tabs_a, tabs_b, masks))
    return tuple(outs)
```

```python
import functools

import numpy as np
import jax
import jax.numpy as jnp
from jax import lax
from jax.experimental import pallas as pl
from jax.experimental.pallas import tpu as pltpu

HEAD_DIM = 128
A_Q_HEADS = 16
A_KV_HEADS = 4
A_GROUP = A_Q_HEADS // A_KV_HEADS
B_GROUPS = ((128, 1), (512, 4), (2048, 16))
N_B_GROUPS = 3
B_HEADS = 8
C_HEADS = 4
GRID_W = 64
ROPE_THETA = 10000.0
EPS = 1e-6
ATTN_SCALE = HEAD_DIM ** -0.5
LOG2E = 1.4426950408889634
NEG_BIG = -1e30
BOUNDED_SCORE_LIMIT = 60.0

F32 = jnp.float32
BF16 = jnp.bfloat16

A_HEAD_PERM = np.concatenate([np.arange(0, 32), np.arange(64, 96), np.arange(32, 64), np.arange(96, 128)])

VMEM_LIMIT_BYTES = 56 * 1024 * 1024

MM_TM = 1024
MM_TN = 1024
MM_TK = 4096
QKV_B_TN = 1536
MLP_UP_TN = 2048
FLASH_TQ = 256
FLASH_TK = 1024
BAND_TILE = 2048
BAND_Q = 128
BAND_SIDE = 64
BAND_MIN_HALO = 256
CROSS_TM = 1024
CROSS_CHAINS = 2


def _params(*sem):
    return pltpu.CompilerParams(dimension_semantics=sem, vmem_limit_bytes=VMEM_LIMIT_BYTES)


def _rms_rows(x, g):
    ms = jnp.mean(x * x, axis=-1, keepdims=True)
    return x * lax.rsqrt(ms + EPS) * g


def _layer_spec(block, layer, imap):
    return pl.BlockSpec((None,) + block, lambda *idx: (layer,) + imap(*idx))


def _mm_norm_kernel(x_ref, g_ref, w_ref, o_ref, xn_ref):
    @pl.when(pl.program_id(1) == 0)
    def _():
        xn_ref[...] = _rms_rows(x_ref[...], g_ref[...]).astype(BF16)

    o_ref[...] = jnp.dot(xn_ref[...], w_ref[...], preferred_element_type=F32).astype(o_ref.dtype)


def mm_norm(x, g, w, layer):
    m, k = x.shape
    n = w.shape[-1]
    tm = min(MM_TM, m)
    tn = min(MM_TN, n)
    return pl.pallas_call(
        _mm_norm_kernel,
        out_shape=jax.ShapeDtypeStruct((m, n), BF16),
        grid=(m // tm, n // tn),
        in_specs=[
            pl.BlockSpec((tm, k), lambda i, j: (i, 0)),
            pl.BlockSpec((1, k), lambda i, j: (0, 0)),
            _layer_spec((k, tn), layer, lambda i, j: (0, j)),
        ],
        out_specs=pl.BlockSpec((tm, tn), lambda i, j: (i, j)),
        scratch_shapes=[pltpu.VMEM((tm, k), BF16)],
        compiler_params=_params("parallel", "arbitrary"),
        name="mm_norm",
    )(x, g.reshape(1, k), w)


def _mm_relu2_kernel(a_ref, w_ref, o_ref):
    acc = jnp.dot(a_ref[...], w_ref[...], preferred_element_type=F32)
    o_ref[...] = jnp.square(jnp.maximum(acc, 0.0)).astype(o_ref.dtype)


def mm_relu2(a, w, layer):
    m, k = a.shape
    n = w.shape[-1]
    tm = min(MM_TM, m)
    tn = min(MLP_UP_TN, n)
    return pl.pallas_call(
        _mm_relu2_kernel,
        out_shape=jax.ShapeDtypeStruct((m, n), BF16),
        grid=(m // tm, n // tn),
        in_specs=[
            pl.BlockSpec((tm, k), lambda i, j: (i, 0)),
            _layer_spec((k, tn), layer, lambda i, j: (0, j)),
        ],
        out_specs=pl.BlockSpec((tm, tn), lambda i, j: (i, j)),
        compiler_params=_params("parallel", "arbitrary"),
        name="mm_relu2",
    )(a, w)


def _mm_res_kernel(a_ref, w_ref, r_ref, *rest, normed, single_k):
    o_ref = rest[-1]

    if single_k:
        o_ref[...] = r_ref[...] + jnp.dot(a_ref[...], w_ref[...], preferred_element_type=F32)
    else:
        @pl.when(pl.program_id(2) == 0)
        def _():
            o_ref[...] = r_ref[...]

        o_ref[...] += jnp.dot(a_ref[...], w_ref[...], preferred_element_type=F32)

    if normed:
        @pl.when(pl.program_id(2) == pl.num_programs(2) - 1)
        def _():
            o_ref[...] = _rms_rows(o_ref[...], rest[0][...])


def mm_res(a, w, layer, res, norm_gain=None):
    m, k = a.shape
    n = w.shape[-1]
    normed = norm_gain is not None
    tm = min(MM_TM // 2, m) if normed else min(MM_TM, m)
    tn = n if normed else min(MM_TN, n)
    tk = min(MM_TK // 2 if normed else MM_TK, k)
    in_specs = [
        pl.BlockSpec((tm, tk), lambda i, j, kk: (i, kk)),
        _layer_spec((tk, tn), layer, lambda i, j, kk: (kk, j)),
        pl.BlockSpec((tm, tn), lambda i, j, kk: (i, j)),
    ]
    args = [a, w, res]
    if normed:
        in_specs.append(pl.BlockSpec((1, n), lambda i, j, kk: (0, 0)))
        args.append(norm_gain.reshape(1, n))
    return pl.pallas_call(
        functools.partial(_mm_res_kernel, normed=normed, single_k=(k == tk)),
        out_shape=jax.ShapeDtypeStruct((m, n), F32),
        grid=(m // tm, n // tn, k // tk),
        in_specs=in_specs,
        out_specs=pl.BlockSpec((tm, tn), lambda i, j, kk: (i, j)),
        compiler_params=_params("parallel", "parallel", "arbitrary"),
        name="mm_res_norm" if normed else "mm_res",
    )(*args)


def _rope_half(x, cos, sin_signed):
    return x * cos + pltpu.roll(x, HEAD_DIM // 2, 1) * sin_signed


def _qkv_a_kernel(x_ref, g_ref, w_ref, qg_ref, kg_ref, cos_ref, sin_ref, qk_ref, v_ref, xn_ref):
    j = pl.program_id(1)
    n_q_tiles = A_Q_HEADS // A_KV_HEADS

    @pl.when(j == 0)
    def _():
        xn_ref[...] = _rms_rows(x_ref[...], g_ref[...]).astype(BF16)

    acc = jnp.dot(xn_ref[...], w_ref[...], preferred_element_type=F32)

    @pl.when(j <= n_q_tiles)
    def _():
        gain = jnp.where(j < n_q_tiles, qg_ref[...] * (ATTN_SCALE * LOG2E), kg_ref[...])
        cos = cos_ref[...]
        sin = sin_ref[...]
        heads = []
        for h in range(A_KV_HEADS):
            sl = slice(h * HEAD_DIM, (h + 1) * HEAD_DIM)
            heads.append(_rope_half(_rms_rows(acc[:, sl], gain), cos, sin).astype(BF16))
        qk_ref[...] = jnp.concatenate(heads, axis=1)

    @pl.when(j == n_q_tiles + 1)
    def _():
        ones = jnp.ones((v_ref.shape[0], HEAD_DIM), BF16)
        for h in range(A_KV_HEADS):
            v_ref[:, 2 * h * HEAD_DIM:(2 * h + 1) * HEAD_DIM] = acc[:, h * HEAD_DIM:(h + 1) * HEAD_DIM].astype(BF16)
            v_ref[:, (2 * h + 1) * HEAD_DIM:(2 * h + 2) * HEAD_DIM] = ones


def qkv_a(x, g, w, layer, q_gain, k_gain, cos, sin_signed, seq):
    m, k = x.shape
    nkv = A_KV_HEADS * HEAD_DIM
    tm = MM_TM
    tn = nkv
    nt = seq // tm
    n_qk_tiles = (A_Q_HEADS + A_KV_HEADS) * HEAD_DIM // tn
    tab_spec = pl.BlockSpec((tm, HEAD_DIM), lambda i, j: (i % nt, 0))
    g_spec = pl.BlockSpec((1, HEAD_DIM), lambda i, j: (0, 0))
    return pl.pallas_call(
        _qkv_a_kernel,
        out_shape=(
            jax.ShapeDtypeStruct((m, n_qk_tiles * tn), BF16),
            jax.ShapeDtypeStruct((m, 2 * nkv), BF16),
        ),
        grid=(m // tm, w.shape[-1] // tn),
        in_specs=[
            pl.BlockSpec((tm, k), lambda i, j: (i, 0)),
            pl.BlockSpec((1, k), lambda i, j: (0, 0)),
            _layer_spec((k, tn), layer, lambda i, j: (0, j)),
            g_spec,
            g_spec,
            tab_spec,
            tab_spec,
        ],
        out_specs=(
            pl.BlockSpec((tm, tn), lambda i, j: (i, jnp.minimum(j, n_qk_tiles - 1))),
            pl.BlockSpec((tm, 2 * nkv), lambda i, j: (i, 0)),
        ),
        scratch_shapes=[pltpu.VMEM((tm, k), BF16)],
        compiler_params=_params("parallel", "arbitrary"),
        name="qkv_a",
    )(x, g.reshape(1, k), w, q_gain.reshape(1, HEAD_DIM), k_gain.reshape(1, HEAD_DIM), cos, sin_signed)


def _stack_q(q_ref, qs_ref, tq):
    for g in range(A_GROUP):
        qs_ref[g * tq:(g + 1) * tq, :] = q_ref[0, :, g * HEAD_DIM:(g + 1) * HEAD_DIM]


def _unstack_o(o, o_ref, tq):
    for g in range(A_GROUP):
        o_ref[0, :, g * HEAD_DIM:(g + 1) * HEAD_DIM] = o[g * tq:(g + 1) * tq, :].astype(o_ref.dtype)


def _flash_a_bounded_kernel(q_ref, k_ref, v_ref, o_ref, qs_ref, acc_ref, *, tq, tk, nk):
    _stack_q(q_ref, qs_ref, tq)
    acc_ref[...] = jnp.zeros(acc_ref.shape, F32)

    def step(c, carry):
        start = pl.multiple_of(c * tk, tk)
        kc = k_ref[0, pl.ds(start, tk), :]
        vc = v_ref[0, pl.ds(start, tk), :]
        s = lax.dot_general(qs_ref[...], kc, (((1,), (1,)), ((), ())), preferred_element_type=F32)
        acc_ref[...] += jnp.dot(jnp.exp2(s).astype(BF16), vc, preferred_element_type=F32)
        return carry

    lax.fori_loop(0, nk, step, 0, unroll=True)
    acc = acc_ref[...]
    _unstack_o(acc[:, :HEAD_DIM] / acc[:, HEAD_DIM:], o_ref, tq)


def _flash_a_online_kernel(q_ref, k_ref, v_ref, o_ref, qs_ref, m_ref, acc_ref, *, tq, tk, nk):
    _stack_q(q_ref, qs_ref, tq)
    m_ref[...] = jnp.full(m_ref.shape, NEG_BIG, F32)
    acc_ref[...] = jnp.zeros(acc_ref.shape, F32)

    def step(c, carry):
        start = pl.multiple_of(c * tk, tk)
        kc = k_ref[0, pl.ds(start, tk), :]
        vc = v_ref[0, pl.ds(start, tk), :]
        s = lax.dot_general(qs_ref[...], kc, (((1,), (1,)), ((), ())), preferred_element_type=F32)
        m_prev = m_ref[...]
        m_new = jnp.maximum(m_prev, jnp.max(s, axis=1, keepdims=True))
        alpha = jnp.exp2(m_prev - m_new)
        p = jnp.exp2(s - m_new)
        acc_ref[...] = alpha * acc_ref[...] + jnp.dot(p.astype(BF16), vc, preferred_element_type=F32)
        m_ref[...] = m_new
        return carry

    lax.fori_loop(0, nk, step, 0)
    acc = acc_ref[...]
    _unstack_o(acc[:, :HEAD_DIM] / acc[:, HEAD_DIM:], o_ref, tq)


def _flash_a_call(qk, v, *, bounded):
    bsz, seq, _ = qk.shape
    nq = A_Q_HEADS * HEAD_DIM
    tq, tk = FLASH_TQ, FLASH_TK
    gw = A_GROUP * HEAD_DIM
    rows = A_GROUP * tq
    scratch = [pltpu.VMEM((rows, HEAD_DIM), BF16)]
    if not bounded:
        scratch.append(pltpu.VMEM((rows, 1), F32))
    scratch.append(pltpu.VMEM((rows, 2 * HEAD_DIM), F32))
    body = _flash_a_bounded_kernel if bounded else _flash_a_online_kernel
    return pl.pallas_call(
        functools.partial(body, tq=tq, tk=tk, nk=seq // tk),
        out_shape=jax.ShapeDtypeStruct((bsz, seq, nq), BF16),
        grid=(bsz, A_KV_HEADS, seq // tq),
        in_specs=[
            pl.BlockSpec((1, tq, gw), lambda b, h, i: (b, i, h)),
            pl.BlockSpec((1, seq, HEAD_DIM), lambda b, h, i: (b, 0, A_Q_HEADS + h)),
            pl.BlockSpec((1, seq, 2 * HEAD_DIM), lambda b, h, i: (b, 0, h)),
        ],
        out_specs=pl.BlockSpec((1, tq, gw), lambda b, h, i: (b, i, h)),
        scratch_shapes=scratch,
        compiler_params=_params("parallel", "parallel", "arbitrary"),
        name="flash_a_bounded" if bounded else "flash_a_online",
    )(qk, qk, v)


def flash_a(qk, v, q_gain, k_gain):
    bound = HEAD_DIM * ATTN_SCALE * jnp.max(jnp.abs(q_gain)) * jnp.max(jnp.abs(k_gain))
    return lax.cond(
        bound <= BOUNDED_SCORE_LIMIT,
        functools.partial(_flash_a_call, bounded=True),
        functools.partial(_flash_a_call, bounded=False),
        qk, v,
    )


def _qkv_b_kernel(x_ref, g_ref, w_ref, cos_ref, sin_ref, o_ref, xn_ref, *, tiles_per_which):
    j = pl.program_id(1)

    @pl.when(j == 0)
    def _():
        xn_ref[...] = _rms_rows(x_ref[...], g_ref[...]).astype(BF16)

    acc = jnp.dot(xn_ref[...], w_ref[...], preferred_element_type=F32)

    rotary = j < 2 * tiles_per_which
    scale = jnp.where(j < tiles_per_which, ATTN_SCALE * LOG2E, 1.0).astype(F32)
    cos = jnp.where(rotary, cos_ref[...] * scale, 1.0)
    sin = jnp.where(rotary, sin_ref[...] * scale, 0.0)
    for h in range(o_ref.shape[1] // HEAD_DIM):
        sl = slice(h * HEAD_DIM, (h + 1) * HEAD_DIM)
        o_ref[:, sl] = _rope_half(acc[:, sl], cos, sin).astype(BF16)


def qkv_b(x, g, w, layer, cos, sin_signed, seq):
    m, k = x.shape
    n = w.shape[-1]
    tm = MM_TM
    tn = QKV_B_TN
    assert (n // 3) % tn == 0
    nt = seq // tm
    tab_spec = pl.BlockSpec((tm, HEAD_DIM), lambda i, j: (i % nt, 0))
    return pl.pallas_call(
        functools.partial(_qkv_b_kernel, tiles_per_which=n // 3 // tn),
        out_shape=jax.ShapeDtypeStruct((m, n), BF16),
        grid=(m // tm, n // tn),
        in_specs=[
            pl.BlockSpec((tm, k), lambda i, j: (i, 0)),
            pl.BlockSpec((1, k), lambda i, j: (0, 0)),
            _layer_spec((k, tn), layer, lambda i, j: (0, j)),
            tab_spec,
            tab_spec,
        ],
        out_specs=pl.BlockSpec((tm, tn), lambda i, j: (i, j)),
        scratch_shapes=[pltpu.VMEM((tm, k), BF16)],
        compiler_params=_params("parallel", "arbitrary"),
        name="qkv_b",
    )(x, g.reshape(1, k), w, cos, sin_signed)


def _band_mask():
    rel = np.arange(2 * BAND_Q)[None, :] - np.arange(BAND_Q)[:, None]
    return np.where((rel >= 0) & (rel <= BAND_Q), 0.0, NEG_BIG).astype(np.float32)


def _band_halo(dil):
    return max(BAND_SIDE * dil, BAND_MIN_HALO)


def _rows(start, size, stride):
    return pl.ds(start, size) if stride == 1 else pl.ds(start, size, stride=stride)


def _band_b_kernel(*refs):
    ng = N_B_GROUPS
    q_refs = refs[0:ng]
    k_refs = refs[ng:4 * ng]
    v_refs = refs[4 * ng:7 * ng]
    mask_ref = refs[7 * ng]
    o_ref = refs[7 * ng + 1]
    scratch = refs[7 * ng + 2:]
    dec = [g for g, (_, dil) in enumerate(B_GROUPS) if dil > 1]
    nd = len(dec)
    qf, kf, vf = (dict(zip(dec, scratch[n * nd:(n + 1) * nd])) for n in range(3))
    og, lg, mg = (scratch[3 * nd + n * ng:3 * nd + (n + 1) * ng] for n in range(3))
    tile = BAND_TILE
    i = pl.program_id(1)

    def window(pieces, lo, h):
        hi = lo + 2 * BAND_Q
        parts = []
        if lo < 0:
            parts.append(pieces[0][0, h + lo:h, :])
        parts.append(pieces[1][0, max(lo, 0):min(hi, tile), :])
        if hi > tile:
            parts.append(pieces[2][0, 0:hi - tile, :])
        return parts[0] if len(parts) == 1 else jnp.concatenate(parts, axis=0)

    for g in dec:
        dil = B_GROUPS[g][1]
        h = _band_halo(dil)
        qf[g][...] = q_refs[g][0].astype(F32)
        for src, dst in ((k_refs, kf[g]), (v_refs, vf[g])):
            dst[0:h, :] = src[3 * g][0].astype(F32)
            dst[h:h + tile, :] = src[3 * g + 1][0].astype(F32)
            dst[h + tile:, :] = src[3 * g + 2][0].astype(F32)

    col = lax.broadcasted_iota(jnp.int32, (1, 2 * BAND_Q), 1)
    lo_row = jnp.where((i == 0) & (col < BAND_SIDE), NEG_BIG, 0.0).astype(F32)
    hi_row = jnp.where((i == pl.num_programs(1) - 1) & (col >= 2 * BAND_Q - BAND_SIDE), NEG_BIG, 0.0).astype(F32)
    mask = mask_ref[...]
    masks = {(False, False): mask, (True, False): mask + lo_row,
             (False, True): mask + hi_row, (True, True): mask + lo_row + hi_row}
    ones = jnp.ones((2 * BAND_Q, HEAD_DIM), BF16)

    for g, (_, dil) in enumerate(B_GROUPS):
        h = _band_halo(dil)
        n_sub = tile // (BAND_Q * dil)
        for mb in range(n_sub):
            for r in range(dil):
                q_rows = _rows(dil * BAND_Q * mb + r, BAND_Q, dil)
                if dil == 1:
                    qb = q_refs[g][0, q_rows, :]
                    kb = window(k_refs[3 * g:3 * g + 3], BAND_Q * mb - BAND_SIDE, h)
                    vb = window(v_refs[3 * g:3 * g + 3], BAND_Q * mb - BAND_SIDE, h)
                else:
                    k_rows = _rows(h + dil * (BAND_Q * mb - BAND_SIDE) + r, 2 * BAND_Q, dil)
                    qb = qf[g][q_rows, :].astype(BF16)
                    kb = kf[g][k_rows, :].astype(BF16)
                    vb = vf[g][k_rows, :].astype(BF16)
                vb = jnp.concatenate([vb, ones], axis=1)
                s = lax.dot_general(qb, kb, (((1,), (1,)), ((), ())), preferred_element_type=F32)
                s = s + masks[(mb == 0, mb == n_sub - 1)]
                m = s.max(axis=1, keepdims=True)
                o = jnp.dot(jnp.exp2(s - m).astype(BF16), vb, preferred_element_type=F32)
                og[g][q_rows, :] = o[:, :HEAD_DIM]
                lg[g][q_rows, :] = o[:, HEAD_DIM:]
                mg[g][q_rows, :] = jnp.broadcast_to(m, (BAND_Q, HEAD_DIM))

    chunk = 2 * BAND_Q
    for c in range(tile // chunk):
        rows = slice(c * chunk, (c + 1) * chunk)
        ms = [mg[g][rows, :] for g in range(ng)]
        m = functools.reduce(jnp.maximum, ms)
        ws = [jnp.exp2(mi - m) for mi in ms]
        num = sum(ws[g] * og[g][rows, :] for g in range(ng))
        den = sum(ws[g] * lg[g][rows, :] for g in range(ng))
        o_ref[0, rows, :] = (num / den).astype(o_ref.dtype)


def band_b(qkv, mask):
    bsz, seq, _ = qkv.shape
    tile = BAND_TILE
    nh = N_B_GROUPS * B_HEADS
    assert all(win == 2 * BAND_SIDE * dil for win, dil in B_GROUPS)
    halos = [_band_halo(dil) for _, dil in B_GROUPS]
    assert all(tile % h == 0 and tile % (BAND_Q * dil) == 0 for h, (_, dil) in zip(halos, B_GROUPS))

    def col(which, g, h):
        return which * nh + g * B_HEADS + h

    def tile_spec(which, g):
        return pl.BlockSpec((1, tile, HEAD_DIM), lambda b, i, h: (b, i, col(which, g, h)))

    def halo_spec(which, g, side):
        per_tile = tile // halos[g]
        last = seq // halos[g] - 1

        def imap(b, i, h):
            blk = i * per_tile - 1 if side < 0 else (i + 1) * per_tile
            return (b, jnp.clip(blk, 0, last), col(which, g, h))
        return pl.BlockSpec((1, halos[g], HEAD_DIM), imap)

    def kv_specs(which):
        return [s for g in range(N_B_GROUPS)
                for s in (halo_spec(which, g, -1), tile_spec(which, g), halo_spec(which, g, +1))]

    in_specs = [tile_spec(0, g) for g in range(N_B_GROUPS)] + kv_specs(1) + kv_specs(2)
    in_specs += [pl.BlockSpec(mask.shape, lambda b, i, h: (0, 0))]
    n_in = len(in_specs) - 1
    dec_halos = [h for h, (_, dil) in zip(halos, B_GROUPS) if dil > 1]
    scratch = [pltpu.VMEM((tile, HEAD_DIM), F32) for _ in dec_halos]
    scratch += [pltpu.VMEM((tile + 2 * h, HEAD_DIM), F32) for _ in range(2) for h in dec_halos]
    scratch += [pltpu.VMEM((tile, HEAD_DIM), F32) for _ in range(3 * N_B_GROUPS)]
    return pl.pallas_call(
        _band_b_kernel,
        out_shape=jax.ShapeDtypeStruct((bsz, seq, B_HEADS * HEAD_DIM), BF16),
        grid=(bsz, seq // tile, B_HEADS),
        in_specs=in_specs,
        out_specs=pl.BlockSpec((1, tile, HEAD_DIM), lambda b, i, h: (b, i, h)),
        scratch_shapes=scratch,
        compiler_params=_params("parallel", "parallel", "arbitrary"),
        name="band_b",
    )(*([qkv] * n_in), mask)


def _cross_kernel(x_ref, g_ref, wq_ref, kv_ref, wo_ref, gn_ref, o_ref, on_ref):
    nkv = C_HEADS * HEAD_DIM
    chain = x_ref.shape[0] // CROSS_CHAINS
    for c in range(CROSS_CHAINS):
        rows = slice(c * chain, (c + 1) * chain)
        x = x_ref[rows, :]
        xn = _rms_rows(x, g_ref[...]).astype(BF16)
        q = (jnp.dot(xn, wq_ref[...], preferred_element_type=F32) * ATTN_SCALE).astype(BF16)
        outs = []
        for h in range(C_HEADS):
            sl = slice(h * HEAD_DIM, (h + 1) * HEAD_DIM)
            kh = kv_ref[0, :, sl]
            vh = kv_ref[0, :, nkv + h * HEAD_DIM:nkv + (h + 1) * HEAD_DIM]
            s = lax.dot_general(q[:, sl], kh, (((1,), (1,)), ((), ())), preferred_element_type=F32)
            p = jnp.exp(s - s.max(axis=1, keepdims=True))
            l = p.sum(axis=1, keepdims=True)
            outs.append((jnp.dot(p.astype(BF16), vh, preferred_element_type=F32) / l).astype(BF16))
        o = jnp.concatenate(outs, axis=1)
        y = x + jnp.dot(o, wo_ref[...], preferred_element_type=F32)
        o_ref[rows, :] = y
        on_ref[rows, :] = _rms_rows(y, gn_ref[...]).astype(BF16)


def cross_block(x, g, wq, kv, wo, layer, next_gain, seq):
    m, d = x.shape
    tm = CROSS_TM
    nt = seq // tm
    nkv = C_HEADS * HEAD_DIM
    mem_len = kv.shape[1]
    return pl.pallas_call(
        _cross_kernel,
        out_shape=(jax.ShapeDtypeStruct((m, d), F32), jax.ShapeDtypeStruct((m, d), BF16)),
        grid=(m // tm,),
        in_specs=[
            pl.BlockSpec((tm, d), lambda i: (i, 0)),
            pl.BlockSpec((1, d), lambda i: (0, 0)),
            _layer_spec((d, nkv), layer, lambda i: (0, 0)),
            pl.BlockSpec((1, mem_len, 2 * nkv), lambda i: (i // nt, 0, 0)),
            _layer_spec((nkv, d), layer, lambda i: (0, 0)),
            pl.BlockSpec((1, d), lambda i: (0, 0)),
        ],
        out_specs=(pl.BlockSpec((tm, d), lambda i: (i, 0)), pl.BlockSpec((tm, d), lambda i: (i, 0))),
        compiler_params=_params("parallel"),
        name="cross_block",
    )(x, g.reshape(1, d), wq, kv, wo, next_gain.reshape(1, d))


def _rope_angles(pos, dim):
    inv = ROPE_THETA ** (-jnp.arange(0, dim, 2, dtype=F32) / dim)
    ang = pos[:, None] * inv[None, :]
    return jnp.concatenate([ang, ang], axis=-1)


def _rope_tables(seq):
    rows = seq // GRID_W
    row = jnp.repeat(jnp.arange(rows, dtype=F32), GRID_W)
    col = jnp.tile(jnp.arange(GRID_W, dtype=F32), rows)
    half = HEAD_DIM // 2
    ang = jnp.concatenate([_rope_angles(row, half), _rope_angles(col, half)], axis=-1)
    low = jnp.arange(HEAD_DIM) < half
    cos_a = jnp.cos(ang)[:, A_HEAD_PERM]
    sin_a = jnp.sin(ang)[:, A_HEAD_PERM]
    sin_a = jnp.where(low, -sin_a, sin_a)
    a1 = _rope_angles(jnp.arange(seq, dtype=F32), HEAD_DIM)
    cos_b = jnp.cos(a1)
    sin_b = jnp.where(low, -jnp.sin(a1), jnp.sin(a1))
    return (cos_a, sin_a), (cos_b, sin_b)


def _permute_a_heads(a_wqkv, a_q_gain, a_k_gain):
    n_qk = (A_Q_HEADS + A_KV_HEADS) * HEAD_DIM
    cols = np.arange(a_wqkv.shape[-1])
    cols[:n_qk] = (cols[:n_qk].reshape(-1, HEAD_DIM)[:, A_HEAD_PERM]).reshape(-1)
    return a_wqkv[..., cols], a_q_gain[..., A_HEAD_PERM], a_k_gain[..., A_HEAD_PERM]


def _trunk(x, mem, wts, tabs_a, tabs_b, masks):
    bsz, seq, d = x.shape
    mem_len = mem.shape[1]
    x = x.reshape(bsz * seq, d)
    mem = mem.reshape(bsz * mem_len, d)
    depth = wts["norm_mix"].shape[0]
    for i in range(depth):
        j = i // 2
        if i % 2 == 0:
            qk, v = qkv_a(x, wts["norm_mix"][i], wts["a_wqkv"], j, wts["a_q_gain"][j], wts["a_k_gain"][j],
                          tabs_a[0], tabs_a[1], seq)
            o = flash_a(qk.reshape(bsz, seq, -1), v.reshape(bsz, seq, -1), wts["a_q_gain"][j], wts["a_k_gain"][j])
            x = mm_res(o.reshape(bsz * seq, -1), wts["a_wo"], j, x)
        else:
            qkv = qkv_b(x, wts["norm_mix"][i], wts["b_wqkv"], j, tabs_b[0], tabs_b[1], seq)
            o = band_b(qkv.reshape(bsz, seq, -1), masks)
            x = mm_res(o.reshape(bsz * seq, -1), wts["b_wo"], j, x)
        kv = mm_norm(mem, wts["norm_mem"][i], wts["c_wkv"], i)
        x, xn = cross_block(x, wts["norm_cross"][i], wts["c_wq"], kv.reshape(bsz, mem_len, -1), wts["c_wo"], i,
                            wts["norm_mlp"][i], seq)
        h = mm_relu2(xn, wts["w_up"], i)
        x = mm_res(h, wts["w_down"], i, x, norm_gain=wts["final_norm"] if i == depth - 1 else None)
    return x.reshape(bsz, seq, d)


def kernel(x_prompt, x_sample, mem_prompt, mem_sample, norm_mix, a_wqkv, a_q_gain, a_k_gain, a_wo, b_wqkv, b_wo, norm_cross, norm_mem, c_wq, c_wkv, c_wo, norm_mlp, w_up, w_down, final_norm):
    a_wqkv, a_q_gain, a_k_gain = _permute_a_heads(a_wqkv, a_q_gain, a_k_gain)
    wts = dict(
        norm_mix=norm_mix, a_wqkv=a_wqkv.astype(BF16), a_q_gain=a_q_gain, a_k_gain=a_k_gain,
        a_wo=a_wo.astype(BF16), b_wqkv=b_wqkv.astype(BF16), b_wo=b_wo.astype(BF16),
        norm_cross=norm_cross, norm_mem=norm_mem, c_wq=c_wq.astype(BF16), c_wkv=c_wkv.astype(BF16),
        c_wo=c_wo.astype(BF16), norm_mlp=norm_mlp, w_up=w_up.astype(BF16), w_down=w_down.astype(BF16),
        final_norm=final_norm,
    )
    masks = jnp.asarray(_band_mask())
    outs = []
    for x, mem in ((x_prompt, mem_prompt), (x_sample, mem_sample)):
        tabs_a, tabs_b = _rope_tables(x.shape[1])
        outs.append(_trunk(x, mem, wts, tabs_a, tabs_b, masks))
    return tuple(outs)
```

```python
import functools

import numpy as np
import jax
import jax.numpy as jnp
from jax import lax
from jax.experimental import pallas as pl
from jax.experimental.pallas import tpu as pltpu

HEAD_DIM = 128
A_Q_HEADS = 16
A_KV_HEADS = 4
A_GROUP = A_Q_HEADS // A_KV_HEADS
B_GROUPS = ((128, 1), (512, 4), (2048, 16))
N_B_GROUPS = 3
B_HEADS = 8
C_HEADS = 4
GRID_W = 64
ROPE_THETA = 10000.0
EPS = 1e-6
ATTN_SCALE = HEAD_DIM ** -0.5
LOG2E = 1.4426950408889634
NEG_BIG = -1e30
BOUNDED_SCORE_LIMIT = 60.0

F32 = jnp.float32
BF16 = jnp.bfloat16

A_HEAD_PERM = np.concatenate([np.arange(0, 32), np.arange(64, 96), np.arange(32, 64), np.arange(96, 128)])

VMEM_LIMIT_BYTES = 56 * 1024 * 1024

MM_TM = 1024
MM_TN = 1024
MM_TK = 4096
QKV_B_TN = 1536
MLP_UP_TN = 2048
FLASH_TQ = 256
FLASH_TK = 1024
BAND_TILE = 2048
BAND_Q = 128
BAND_SIDE = 64
BAND_MIN_HALO = 256
BAND_STAGE = 4
CROSS_TM = 1024
CROSS_CHAINS = 2


def _params(*sem):
    return pltpu.CompilerParams(dimension_semantics=sem, vmem_limit_bytes=VMEM_LIMIT_BYTES)


def _rms_rows(x, g):
    ms = jnp.mean(x * x, axis=-1, keepdims=True)
    return x * lax.rsqrt(ms + EPS) * g


def _layer_spec(block, layer, imap):
    return pl.BlockSpec((None,) + block, lambda *idx: (layer,) + imap(*idx))


def _mm_norm_kernel(x_ref, g_ref, w_ref, o_ref, xn_ref):
    @pl.when(pl.program_id(1) == 0)
    def _():
        xn_ref[...] = _rms_rows(x_ref[...], g_ref[...]).astype(BF16)

    o_ref[...] = jnp.dot(xn_ref[...], w_ref[...], preferred_element_type=F32).astype(o_ref.dtype)


def mm_norm(x, g, w, layer):
    m, k = x.shape
    n = w.shape[-1]
    tm = min(MM_TM, m)
    tn = min(MM_TN, n)
    return pl.pallas_call(
        _mm_norm_kernel,
        out_shape=jax.ShapeDtypeStruct((m, n), BF16),
        grid=(m // tm, n // tn),
        in_specs=[
            pl.BlockSpec((tm, k), lambda i, j: (i, 0)),
            pl.BlockSpec((1, k), lambda i, j: (0, 0)),
            _layer_spec((k, tn), layer, lambda i, j: (0, j)),
        ],
        out_specs=pl.BlockSpec((tm, tn), lambda i, j: (i, j)),
        scratch_shapes=[pltpu.VMEM((tm, k), BF16)],
        compiler_params=_params("parallel", "arbitrary"),
        name="mm_norm",
    )(x, g.reshape(1, k), w)


def _mm_relu2_kernel(a_ref, w_ref, o_ref):
    acc = jnp.dot(a_ref[...], w_ref[...], preferred_element_type=F32)
    o_ref[...] = jnp.square(jnp.maximum(acc, 0.0)).astype(o_ref.dtype)


def mm_relu2(a, w, layer):
    m, k = a.shape
    n = w.shape[-1]
    tm = min(MM_TM, m)
    tn = min(MLP_UP_TN, n)
    return pl.pallas_call(
        _mm_relu2_kernel,
        out_shape=jax.ShapeDtypeStruct((m, n), BF16),
        grid=(m // tm, n // tn),
        in_specs=[
            pl.BlockSpec((tm, k), lambda i, j: (i, 0)),
            _layer_spec((k, tn), layer, lambda i, j: (0, j)),
        ],
        out_specs=pl.BlockSpec((tm, tn), lambda i, j: (i, j)),
        compiler_params=_params("parallel", "arbitrary"),
        name="mm_relu2",
    )(a, w)


def _mm_res_kernel(a_ref, w_ref, r_ref, *rest, normed, single_k):
    o_ref = rest[-1]

    if single_k:
        o_ref[...] = r_ref[...] + jnp.dot(a_ref[...], w_ref[...], preferred_element_type=F32)
    else:
        @pl.when(pl.program_id(2) == 0)
        def _():
            o_ref[...] = r_ref[...]

        o_ref[...] += jnp.dot(a_ref[...], w_ref[...], preferred_element_type=F32)

    if normed:
        @pl.when(pl.program_id(2) == pl.num_programs(2) - 1)
        def _():
            o_ref[...] = _rms_rows(o_ref[...], rest[0][...])


def mm_res(a, w, layer, res, norm_gain=None):
    m, k = a.shape
    n = w.shape[-1]
    normed = norm_gain is not None
    tm = min(MM_TM // 2, m) if normed else min(MM_TM, m)
    tn = n if normed else min(MM_TN, n)
    tk = min(MM_TK // 2 if normed else MM_TK, k)
    in_specs = [
        pl.BlockSpec((tm, tk), lambda i, j, kk: (i, kk)),
        _layer_spec((tk, tn), layer, lambda i, j, kk: (kk, j)),
        pl.BlockSpec((tm, tn), lambda i, j, kk: (i, j)),
    ]
    args = [a, w, res]
    if normed:
        in_specs.append(pl.BlockSpec((1, n), lambda i, j, kk: (0, 0)))
        args.append(norm_gain.reshape(1, n))
    return pl.pallas_call(
        functools.partial(_mm_res_kernel, normed=normed, single_k=(k == tk)),
        out_shape=jax.ShapeDtypeStruct((m, n), F32),
        grid=(m // tm, n // tn, k // tk),
        in_specs=in_specs,
        out_specs=pl.BlockSpec((tm, tn), lambda i, j, kk: (i, j)),
        compiler_params=_params("parallel", "parallel", "arbitrary"),
        name="mm_res_norm" if normed else "mm_res",
    )(*args)


def _rope_half(x, cos, sin_signed):
    return x * cos + pltpu.roll(x, HEAD_DIM // 2, 1) * sin_signed


def _qkv_a_kernel(x_ref, g_ref, w_ref, qg_ref, kg_ref, cos_ref, sin_ref, qk_ref, v_ref, xn_ref):
    j = pl.program_id(1)
    n_q_tiles = A_Q_HEADS // A_KV_HEADS

    @pl.when(j == 0)
    def _():
        xn_ref[...] = _rms_rows(x_ref[...], g_ref[...]).astype(BF16)

    acc = jnp.dot(xn_ref[...], w_ref[...], preferred_element_type=F32)

    @pl.when(j <= n_q_tiles)
    def _():
        gain = jnp.where(j < n_q_tiles, qg_ref[...] * (ATTN_SCALE * LOG2E), kg_ref[...])
        cos = cos_ref[...]
        sin = sin_ref[...]
        heads = []
        for h in range(A_KV_HEADS):
            sl = slice(h * HEAD_DIM, (h + 1) * HEAD_DIM)
            heads.append(_rope_half(_rms_rows(acc[:, sl], gain), cos, sin).astype(BF16))
        qk_ref[...] = jnp.concatenate(heads, axis=1)

    @pl.when(j == n_q_tiles + 1)
    def _():
        ones = jnp.ones((v_ref.shape[0], HEAD_DIM), BF16)
        for h in range(A_KV_HEADS):
            v_ref[:, 2 * h * HEAD_DIM:(2 * h + 1) * HEAD_DIM] = acc[:, h * HEAD_DIM:(h + 1) * HEAD_DIM].astype(BF16)
            v_ref[:, (2 * h + 1) * HEAD_DIM:(2 * h + 2) * HEAD_DIM] = ones


def qkv_a(x, g, w, layer, q_gain, k_gain, cos, sin_signed, seq):
    m, k = x.shape
    nkv = A_KV_HEADS * HEAD_DIM
    tm = MM_TM
    tn = nkv
    nt = seq // tm
    n_qk_tiles = (A_Q_HEADS + A_KV_HEADS) * HEAD_DIM // tn
    tab_spec = pl.BlockSpec((tm, HEAD_DIM), lambda i, j: (i % nt, 0))
    g_spec = pl.BlockSpec((1, HEAD_DIM), lambda i, j: (0, 0))
    return pl.pallas_call(
        _qkv_a_kernel,
        out_shape=(
            jax.ShapeDtypeStruct((m, n_qk_tiles * tn), BF16),
            jax.ShapeDtypeStruct((m, 2 * nkv), BF16),
        ),
        grid=(m // tm, w.shape[-1] // tn),
        in_specs=[
            pl.BlockSpec((tm, k), lambda i, j: (i, 0)),
            pl.BlockSpec((1, k), lambda i, j: (0, 0)),
            _layer_spec((k, tn), layer, lambda i, j: (0, j)),
            g_spec,
            g_spec,
            tab_spec,
            tab_spec,
        ],
        out_specs=(
            pl.BlockSpec((tm, tn), lambda i, j: (i, jnp.minimum(j, n_qk_tiles - 1))),
            pl.BlockSpec((tm, 2 * nkv), lambda i, j: (i, 0)),
        ),
        scratch_shapes=[pltpu.VMEM((tm, k), BF16)],
        compiler_params=_params("parallel", "arbitrary"),
        name="qkv_a",
    )(x, g.reshape(1, k), w, q_gain.reshape(1, HEAD_DIM), k_gain.reshape(1, HEAD_DIM), cos, sin_signed)


def _stack_q(q_ref, qs_ref, tq):
    for g in range(A_GROUP):
        qs_ref[g * tq:(g + 1) * tq, :] = q_ref[0, :, g * HEAD_DIM:(g + 1) * HEAD_DIM]


def _unstack_o(o, o_ref, tq):
    for g in range(A_GROUP):
        o_ref[0, :, g * HEAD_DIM:(g + 1) * HEAD_DIM] = o[g * tq:(g + 1) * tq, :].astype(o_ref.dtype)


def _flash_a_bounded_kernel(q_ref, k_ref, v_ref, o_ref, qs_ref, acc_ref, *, tq, tk, nk):
    _stack_q(q_ref, qs_ref, tq)
    acc_ref[...] = jnp.zeros(acc_ref.shape, F32)

    def step(c, carry):
        start = pl.multiple_of(c * tk, tk)
        kc = k_ref[0, pl.ds(start, tk), :]
        vc = v_ref[0, pl.ds(start, tk), :]
        s = lax.dot_general(qs_ref[...], kc, (((1,), (1,)), ((), ())), preferred_element_type=F32)
        acc_ref[...] += jnp.dot(jnp.exp2(s).astype(BF16), vc, preferred_element_type=F32)
        return carry

    lax.fori_loop(0, nk, step, 0, unroll=True)
    acc = acc_ref[...]
    _unstack_o(acc[:, :HEAD_DIM] / acc[:, HEAD_DIM:], o_ref, tq)


def _flash_a_online_kernel(q_ref, k_ref, v_ref, o_ref, qs_ref, m_ref, acc_ref, *, tq, tk, nk):
    _stack_q(q_ref, qs_ref, tq)
    m_ref[...] = jnp.full(m_ref.shape, NEG_BIG, F32)
    acc_ref[...] = jnp.zeros(acc_ref.shape, F32)

    def step(c, carry):
        start = pl.multiple_of(c * tk, tk)
        kc = k_ref[0, pl.ds(start, tk), :]
        vc = v_ref[0, pl.ds(start, tk), :]
        s = lax.dot_general(qs_ref[...], kc, (((1,), (1,)), ((), ())), preferred_element_type=F32)
        m_prev = m_ref[...]
        m_new = jnp.maximum(m_prev, jnp.max(s, axis=1, keepdims=True))
        alpha = jnp.exp2(m_prev - m_new)
        p = jnp.exp2(s - m_new)
        acc_ref[...] = alpha * acc_ref[...] + jnp.dot(p.astype(BF16), vc, preferred_element_type=F32)
        m_ref[...] = m_new
        return carry

    lax.fori_loop(0, nk, step, 0)
    acc = acc_ref[...]
    _unstack_o(acc[:, :HEAD_DIM] / acc[:, HEAD_DIM:], o_ref, tq)


def _flash_a_call(qk, v, *, bounded):
    bsz, seq, _ = qk.shape
    nq = A_Q_HEADS * HEAD_DIM
    tq, tk = FLASH_TQ, FLASH_TK
    gw = A_GROUP * HEAD_DIM
    rows = A_GROUP * tq
    scratch = [pltpu.VMEM((rows, HEAD_DIM), BF16)]
    if not bounded:
        scratch.append(pltpu.VMEM((rows, 1), F32))
    scratch.append(pltpu.VMEM((rows, 2 * HEAD_DIM), F32))
    body = _flash_a_bounded_kernel if bounded else _flash_a_online_kernel
    return pl.pallas_call(
        functools.partial(body, tq=tq, tk=tk, nk=seq // tk),
        out_shape=jax.ShapeDtypeStruct((bsz, seq, nq), BF16),
        grid=(bsz, A_KV_HEADS, seq // tq),
        in_specs=[
            pl.BlockSpec((1, tq, gw), lambda b, h, i: (b, i, h)),
            pl.BlockSpec((1, seq, HEAD_DIM), lambda b, h, i: (b, 0, A_Q_HEADS + h)),
            pl.BlockSpec((1, seq, 2 * HEAD_DIM), lambda b, h, i: (b, 0, h)),
        ],
        out_specs=pl.BlockSpec((1, tq, gw), lambda b, h, i: (b, i, h)),
        scratch_shapes=scratch,
        compiler_params=_params("parallel", "parallel", "arbitrary"),
        name="flash_a_bounded" if bounded else "flash_a_online",
    )(qk, qk, v)


def flash_a(qk, v, q_gain, k_gain):
    bound = HEAD_DIM * ATTN_SCALE * jnp.max(jnp.abs(q_gain)) * jnp.max(jnp.abs(k_gain))
    return lax.cond(
        bound <= BOUNDED_SCORE_LIMIT,
        functools.partial(_flash_a_call, bounded=True),
        functools.partial(_flash_a_call, bounded=False),
        qk, v,
    )


def _qkv_b_kernel(x_ref, g_ref, w_ref, cos_ref, sin_ref, o_ref, xn_ref, *, tiles_per_which):
    j = pl.program_id(1)

    @pl.when(j == 0)
    def _():
        xn_ref[...] = _rms_rows(x_ref[...], g_ref[...]).astype(BF16)

    acc = jnp.dot(xn_ref[...], w_ref[...], preferred_element_type=F32)

    rotary = j < 2 * tiles_per_which
    scale = jnp.where(j < tiles_per_which, ATTN_SCALE * LOG2E, 1.0).astype(F32)
    cos = jnp.where(rotary, cos_ref[...] * scale, 1.0)
    sin = jnp.where(rotary, sin_ref[...] * scale, 0.0)
    for h in range(o_ref.shape[1] // HEAD_DIM):
        sl = slice(h * HEAD_DIM, (h + 1) * HEAD_DIM)
        o_ref[:, sl] = _rope_half(acc[:, sl], cos, sin).astype(BF16)


def qkv_b(x, g, w, layer, cos, sin_signed, seq):
    m, k = x.shape
    n = w.shape[-1]
    tm = MM_TM
    tn = QKV_B_TN
    assert (n // 3) % tn == 0
    nt = seq // tm
    tab_spec = pl.BlockSpec((tm, HEAD_DIM), lambda i, j: (i % nt, 0))
    return pl.pallas_call(
        functools.partial(_qkv_b_kernel, tiles_per_which=n // 3 // tn),
        out_shape=jax.ShapeDtypeStruct((m, n), BF16),
        grid=(m // tm, n // tn),
        in_specs=[
            pl.BlockSpec((tm, k), lambda i, j: (i, 0)),
            pl.BlockSpec((1, k), lambda i, j: (0, 0)),
            _layer_spec((k, tn), layer, lambda i, j: (0, j)),
            tab_spec,
            tab_spec,
        ],
        out_specs=pl.BlockSpec((tm, tn), lambda i, j: (i, j)),
        scratch_shapes=[pltpu.VMEM((tm, k), BF16)],
        compiler_params=_params("parallel", "arbitrary"),
        name="qkv_b",
    )(x, g.reshape(1, k), w, cos, sin_signed)


def _band_mask():
    rel = np.arange(2 * BAND_Q)[None, :] - np.arange(BAND_Q)[:, None]
    return np.where((rel >= 0) & (rel <= BAND_Q), 0.0, NEG_BIG).astype(np.float32)


def _band_halo(dil):
    return max(BAND_SIDE * dil, BAND_MIN_HALO)


def _rows(start, size, stride):
    return pl.ds(start, size) if stride == 1 else pl.ds(start, size, stride=stride)


def _band_b_kernel(*refs):
    ng = N_B_GROUPS
    q_refs = refs[0:ng]
    k_refs = refs[ng:4 * ng]
    v_refs = refs[4 * ng:7 * ng]
    mask_ref = refs[7 * ng]
    o_ref = refs[7 * ng + 1]
    scratch = refs[7 * ng + 2:]
    dec = [g for g, (_, dil) in enumerate(B_GROUPS) if dil > 1]
    nd = len(dec)
    qf, kf, vf = (dict(zip(dec, scratch[n * nd:(n + 1) * nd])) for n in range(3))
    og, lg, mg = (scratch[3 * nd + n * ng:3 * nd + (n + 1) * ng] for n in range(3))
    tile = BAND_TILE
    i = pl.program_id(1)

    def window(pieces, lo, h):
        hi = lo + 2 * BAND_Q
        parts = []
        if lo < 0:
            parts.append(pieces[0][0, h + lo:h, :])
        parts.append(pieces[1][0, max(lo, 0):min(hi, tile), :])
        if hi > tile:
            parts.append(pieces[2][0, 0:hi - tile, :])
        return parts[0] if len(parts) == 1 else jnp.concatenate(parts, axis=0)

    for g in dec:
        dil = B_GROUPS[g][1]
        h = _band_halo(dil)
        qf[g][...] = q_refs[g][0].astype(F32)
        for src, dst in ((k_refs, kf[g]), (v_refs, vf[g])):
            dst[0:h, :] = src[3 * g][0].astype(F32)
            dst[h:h + tile, :] = src[3 * g + 1][0].astype(F32)
            dst[h + tile:, :] = src[3 * g + 2][0].astype(F32)

    staged = [g for g in dec if B_GROUPS[g][1] > BAND_STAGE]
    base = 3 * nd + 3 * ng
    q4, k4, v4 = (dict(zip(staged, scratch[base + n * len(staged):base + (n + 1) * len(staged)])) for n in range(3))
    for g in staged:
        for src, dst in ((qf[g], q4[g]), (kf[g], k4[g]), (vf[g], v4[g])):
            n = src.shape[0] // BAND_STAGE
            for a in range(BAND_STAGE):
                dst[a * n:(a + 1) * n, :] = src[pl.ds(a, n, stride=BAND_STAGE), :]

    col = lax.broadcasted_iota(jnp.int32, (1, 2 * BAND_Q), 1)
    lo_row = jnp.where((i == 0) & (col < BAND_SIDE), NEG_BIG, 0.0).astype(F32)
    hi_row = jnp.where((i == pl.num_programs(1) - 1) & (col >= 2 * BAND_Q - BAND_SIDE), NEG_BIG, 0.0).astype(F32)
    mask = mask_ref[...]
    masks = {(False, False): mask, (True, False): mask + lo_row,
             (False, True): mask + hi_row, (True, True): mask + lo_row + hi_row}
    ones = jnp.ones((2 * BAND_Q, HEAD_DIM), BF16)

    for g, (_, dil) in enumerate(B_GROUPS):
        h = _band_halo(dil)
        n_sub = tile // (BAND_Q * dil)
        for mb in range(n_sub):
            for r in range(dil):
                q_rows = _rows(dil * BAND_Q * mb + r, BAND_Q, dil)
                if dil == 1:
                    qb = q_refs[g][0, q_rows, :]
                    kb = window(k_refs[3 * g:3 * g + 3], BAND_Q * mb - BAND_SIDE, h)
                    vb = window(v_refs[3 * g:3 * g + 3], BAND_Q * mb - BAND_SIDE, h)
                elif g in staged:
                    a, b = r % BAND_STAGE, r // BAND_STAGE
                    d2 = dil // BAND_STAGE
                    nq, nk = q4[g].shape[0] // BAND_STAGE, k4[g].shape[0] // BAND_STAGE
                    q4_rows = pl.ds(a * nq + d2 * BAND_Q * mb + b, BAND_Q, stride=d2)
                    k4_rows = pl.ds(a * nk + (h + dil * (BAND_Q * mb - BAND_SIDE)) // BAND_STAGE + b, 2 * BAND_Q, stride=d2)
                    qb = q4[g][q4_rows, :].astype(BF16)
                    kb = k4[g][k4_rows, :].astype(BF16)
                    vb = v4[g][k4_rows, :].astype(BF16)
                else:
                    k_rows = _rows(h + dil * (BAND_Q * mb - BAND_SIDE) + r, 2 * BAND_Q, dil)
                    qb = qf[g][q_rows, :].astype(BF16)
                    kb = kf[g][k_rows, :].astype(BF16)
                    vb = vf[g][k_rows, :].astype(BF16)
                vb = jnp.concatenate([vb, ones], axis=1)
                s = lax.dot_general(qb, kb, (((1,), (1,)), ((), ())), preferred_element_type=F32)
                s = s + masks[(mb == 0, mb == n_sub - 1)]
                m = s.max(axis=1, keepdims=True)
                o = jnp.dot(jnp.exp2(s - m).astype(BF16), vb, preferred_element_type=F32)
                og[g][q_rows, :] = o[:, :HEAD_DIM]
                lg[g][q_rows, :] = o[:, HEAD_DIM:]
                mg[g][q_rows, :] = jnp.broadcast_to(m, (BAND_Q, HEAD_DIM))

    chunk = 2 * BAND_Q
    for c in range(tile // chunk):
        rows = slice(c * chunk, (c + 1) * chunk)
        ms = [mg[g][rows, :] for g in range(ng)]
        m = functools.reduce(jnp.maximum, ms)
        ws = [jnp.exp2(mi - m) for mi in ms]
        num = sum(ws[g] * og[g][rows, :] for g in range(ng))
        den = sum(ws[g] * lg[g][rows, :] for g in range(ng))
        o_ref[0, rows, :] = (num / den).astype(o_ref.dtype)


def band_b(qkv, mask):
    bsz, seq, _ = qkv.shape
    tile = BAND_TILE
    nh = N_B_GROUPS * B_HEADS
    assert all(win == 2 * BAND_SIDE * dil for win, dil in B_GROUPS)
    halos = [_band_halo(dil) for _, dil in B_GROUPS]
    assert all(tile % h == 0 and tile % (BAND_Q * dil) == 0 for h, (_, dil) in zip(halos, B_GROUPS))

    def col(which, g, h):
        return which * nh + g * B_HEADS + h

    def tile_spec(which, g):
        return pl.BlockSpec((1, tile, HEAD_DIM), lambda b, i, h: (b, i, col(which, g, h)))

    def halo_spec(which, g, side):
        per_tile = tile // halos[g]
        last = seq // halos[g] - 1

        def imap(b, i, h):
            blk = i * per_tile - 1 if side < 0 else (i + 1) * per_tile
            return (b, jnp.clip(blk, 0, last), col(which, g, h))
        return pl.BlockSpec((1, halos[g], HEAD_DIM), imap)

    def kv_specs(which):
        return [s for g in range(N_B_GROUPS)
                for s in (halo_spec(which, g, -1), tile_spec(which, g), halo_spec(which, g, +1))]

    in_specs = [tile_spec(0, g) for g in range(N_B_GROUPS)] + kv_specs(1) + kv_specs(2)
    in_specs += [pl.BlockSpec(mask.shape, lambda b, i, h: (0, 0))]
    n_in = len(in_specs) - 1
    dec_halos = [h for h, (_, dil) in zip(halos, B_GROUPS) if dil > 1]
    scratch = [pltpu.VMEM((tile, HEAD_DIM), F32) for _ in dec_halos]
    scratch += [pltpu.VMEM((tile + 2 * h, HEAD_DIM), F32) for _ in range(2) for h in dec_halos]
    scratch += [pltpu.VMEM((tile, HEAD_DIM), F32) for _ in range(3 * N_B_GROUPS)]
    stage_halos = [h for h, (_, dil) in zip(halos, B_GROUPS) if dil > BAND_STAGE]
    scratch += [pltpu.VMEM((tile, HEAD_DIM), F32) for _ in stage_halos]
    scratch += [pltpu.VMEM((tile + 2 * h, HEAD_DIM), F32) for _ in range(2) for h in stage_halos]
    return pl.pallas_call(
        _band_b_kernel,
        out_shape=jax.ShapeDtypeStruct((bsz, seq, B_HEADS * HEAD_DIM), BF16),
        grid=(bsz, seq // tile, B_HEADS),
        in_specs=in_specs,
        out_specs=pl.BlockSpec((1, tile, HEAD_DIM), lambda b, i, h: (b, i, h)),
        scratch_shapes=scratch,
        compiler_params=_params("parallel", "parallel", "arbitrary"),
        name="band_b",
    )(*([qkv] * n_in), mask)


def _cross_kernel(x_ref, g_ref, wq_ref, kv_ref, wo_ref, gn_ref, o_ref, on_ref):
    nkv = C_HEADS * HEAD_DIM
    chain = x_ref.shape[0] // CROSS_CHAINS
    for c in range(CROSS_CHAINS):
        rows = slice(c * chain, (c + 1) * chain)
        x = x_ref[rows, :]
        xn = _rms_rows(x, g_ref[...]).astype(BF16)
        q = (jnp.dot(xn, wq_ref[...], preferred_element_type=F32) * ATTN_SCALE).astype(BF16)
        outs = []
        for h in range(C_HEADS):
            sl = slice(h * HEAD_DIM, (h + 1) * HEAD_DIM)
            kh = kv_ref[0, :, sl]
            vh = kv_ref[0, :, nkv + h * HEAD_DIM:nkv + (h + 1) * HEAD_DIM]
            s = lax.dot_general(q[:, sl], kh, (((1,), (1,)), ((), ())), preferred_element_type=F32)
            p = jnp.exp(s - s.max(axis=1, keepdims=True))
            l = p.sum(axis=1, keepdims=True)
            outs.append((jnp.dot(p.astype(BF16), vh, preferred_element_type=F32) / l).astype(BF16))
        o = jnp.concatenate(outs, axis=1)
        y = x + jnp.dot(o, wo_ref[...], preferred_element_type=F32)
        o_ref[rows, :] = y
        on_ref[rows, :] = _rms_rows(y, gn_ref[...]).astype(BF16)


def cross_block(x, g, wq, kv, wo, layer, next_gain, seq):
    m, d = x.shape
    tm = CROSS_TM
    nt = seq // tm
    nkv = C_HEADS * HEAD_DIM
    mem_len = kv.shape[1]
    return pl.pallas_call(
        _cross_kernel,
        out_shape=(jax.ShapeDtypeStruct((m, d), F32), jax.ShapeDtypeStruct((m, d), BF16)),
        grid=(m // tm,),
        in_specs=[
            pl.BlockSpec((tm, d), lambda i: (i, 0)),
            pl.BlockSpec((1, d), lambda i: (0, 0)),
            _layer_spec((d, nkv), layer, lambda i: (0, 0)),
            pl.BlockSpec((1, mem_len, 2 * nkv), lambda i: (i // nt, 0, 0)),
            _layer_spec((nkv, d), layer, lambda i: (0, 0)),
            pl.BlockSpec((1, d), lambda i: (0, 0)),
        ],
        out_specs=(pl.BlockSpec((tm, d), lambda i: (i, 0)), pl.BlockSpec((tm, d), lambda i: (i, 0))),
        compiler_params=_params("parallel"),
        name="cross_block",
    )(x, g.reshape(1, d), wq, kv, wo, next_gain.reshape(1, d))


def _rope_angles(pos, dim):
    inv = ROPE_THETA ** (-jnp.arange(0, dim, 2, dtype=F32) / dim)
    ang = pos[:, None] * inv[None, :]
    return jnp.concatenate([ang, ang], axis=-1)


def _rope_tables(seq):
    rows = seq // GRID_W
    row = jnp.repeat(jnp.arange(rows, dtype=F32), GRID_W)
    col = jnp.tile(jnp.arange(GRID_W, dtype=F32), rows)
    half = HEAD_DIM // 2
    ang = jnp.concatenate([_rope_angles(row, half), _rope_angles(col, half)], axis=-1)
    low = jnp.arange(HEAD_DIM) < half
    cos_a = jnp.cos(ang)[:, A_HEAD_PERM]
    sin_a = jnp.sin(ang)[:, A_HEAD_PERM]
    sin_a = jnp.where(low, -sin_a, sin_a)
    a1 = _rope_angles(jnp.arange(seq, dtype=F32), HEAD_DIM)
    cos_b = jnp.cos(a1)
    sin_b = jnp.where(low, -jnp.sin(a1), jnp.sin(a1))
    return (cos_a, sin_a), (cos_b, sin_b)


def _permute_a_heads(a_wqkv, a_q_gain, a_k_gain):
    n_qk = (A_Q_HEADS + A_KV_HEADS) * HEAD_DIM
    cols = np.arange(a_wqkv.shape[-1])
    cols[:n_qk] = (cols[:n_qk].reshape(-1, HEAD_DIM)[:, A_HEAD_PERM]).reshape(-1)
    return a_wqkv[..., cols], a_q_gain[..., A_HEAD_PERM], a_k_gain[..., A_HEAD_PERM]


def _trunk(x, mem, wts, tabs_a, tabs_b, masks):
    bsz, seq, d = x.shape
    mem_len = mem.shape[1]
    x = x.reshape(bsz * seq, d)
    mem = mem.reshape(bsz * mem_len, d)
    depth = wts["norm_mix"].shape[0]
    for i in range(depth):
        j = i // 2
        if i % 2 == 0:
            qk, v = qkv_a(x, wts["norm_mix"][i], wts["a_wqkv"], j, wts["a_q_gain"][j], wts["a_k_gain"][j],
                          tabs_a[0], tabs_a[1], seq)
            o = flash_a(qk.reshape(bsz, seq, -1), v.reshape(bsz, seq, -1), wts["a_q_gain"][j], wts["a_k_gain"][j])
            x = mm_res(o.reshape(bsz * seq, -1), wts["a_wo"], j, x)
        else:
            qkv = qkv_b(x, wts["norm_mix"][i], wts["b_wqkv"], j, tabs_b[0], tabs_b[1], seq)
            o = band_b(qkv.reshape(bsz, seq, -1), masks)
            x = mm_res(o.reshape(bsz * seq, -1), wts["b_wo"], j, x)
        kv = mm_norm(mem, wts["norm_mem"][i], wts["c_wkv"], i)
        x, xn = cross_block(x, wts["norm_cross"][i], wts["c_wq"], kv.reshape(bsz, mem_len, -1), wts["c_wo"], i,
                            wts["norm_mlp"][i], seq)
        h = mm_relu2(xn, wts["w_up"], i)
        x = mm_res(h, wts["w_down"], i, x, norm_gain=wts["final_norm"] if i == depth - 1 else None)
    return x.reshape(bsz, seq, d)


def kernel(x_prompt, x_sample, mem_prompt, mem_sample, norm_mix, a_wqkv, a_q_gain, a_k_gain, a_wo, b_wqkv, b_wo, norm_cross, norm_mem, c_wq, c_wkv, c_wo, norm_mlp, w_up, w_down, final_norm):
    a_wqkv, a_q_gain, a_k_gain = _permute_a_heads(a_wqkv, a_q_gain, a_k_gain)
    wts = dict(
        norm_mix=norm_mix, a_wqkv=a_wqkv.astype(BF16), a_q_gain=a_q_gain, a_k_gain=a_k_gain,
        a_wo=a_wo.astype(BF16), b_wqkv=b_wqkv.astype(BF16), b_wo=b_wo.astype(BF16),
        norm_cross=norm_cross, norm_mem=norm_mem, c_wq=c_wq.astype(BF16), c_wkv=c_wkv.astype(BF16),
        c_wo=c_wo.astype(BF16), norm_mlp=norm_mlp, w_up=w_up.astype(BF16), w_down=w_down.astype(BF16),
        final_norm=final_norm,
    )
    masks = jnp.asarray(_band_mask())
    outs = []
    for x, mem in ((x_prompt, mem_prompt), (x_sample, mem_sample)):
        tabs_a, tabs_b = _rope_tables(x.shape[1])
        outs.append(_trunk(x, mem, wts, tabs_a, tabs_b, masks))
    return tuple(outs)
```

```python
import functools

import numpy as np
import jax
import jax.numpy as jnp
from jax import lax
from jax.experimental import pallas as pl
from jax.experimental.pallas import tpu as pltpu

HEAD_DIM = 128
A_Q_HEADS = 16
A_KV_HEADS = 4
A_GROUP = A_Q_HEADS // A_KV_HEADS
B_GROUPS = ((128, 1), (512, 4), (2048, 16))
N_B_GROUPS = 3
B_HEADS = 8
C_HEADS = 4
GRID_W = 64
ROPE_THETA = 10000.0
EPS = 1e-6
ATTN_SCALE = HEAD_DIM ** -0.5
LOG2E = 1.4426950408889634
NEG_BIG = -1e30
BOUNDED_SCORE_LIMIT = 60.0

F32 = jnp.float32
BF16 = jnp.bfloat16

A_HEAD_PERM = np.concatenate([np.arange(0, 32), np.arange(64, 96), np.arange(32, 64), np.arange(96, 128)])

VMEM_LIMIT_BYTES = 56 * 1024 * 1024

MM_TM = 1024
MM_TN = 1024
MM_TK = 4096
QKV_B_TN = 1536
MLP_UP_TN = 2048
FLASH_TQ = 256
FLASH_TK = 1024
BAND_TILE = 2048
BAND_Q = 128
BAND_SIDE = 64
BAND_MIN_HALO = 256
BAND_STAGE = 4
CROSS_TM = 1024
CROSS_CHAINS = 2


def _params(*sem):
    return pltpu.CompilerParams(dimension_semantics=sem, vmem_limit_bytes=VMEM_LIMIT_BYTES)


def _rms_rows(x, g):
    ms = jnp.mean(x * x, axis=-1, keepdims=True)
    return x * lax.rsqrt(ms + EPS) * g


def _layer_spec(block, layer, imap):
    return pl.BlockSpec((None,) + block, lambda *idx: (layer,) + imap(*idx))


def _mm_norm_kernel(x_ref, g_ref, w_ref, o_ref, xn_ref):
    @pl.when(pl.program_id(1) == 0)
    def _():
        xn_ref[...] = _rms_rows(x_ref[...], g_ref[...]).astype(BF16)

    o_ref[...] = jnp.dot(xn_ref[...], w_ref[...], preferred_element_type=F32).astype(o_ref.dtype)


def mm_norm(x, g, w, layer):
    m, k = x.shape
    n = w.shape[-1]
    tm = min(MM_TM, m)
    tn = min(MM_TN, n)
    return pl.pallas_call(
        _mm_norm_kernel,
        out_shape=jax.ShapeDtypeStruct((m, n), BF16),
        grid=(m // tm, n // tn),
        in_specs=[
            pl.BlockSpec((tm, k), lambda i, j: (i, 0)),
            pl.BlockSpec((1, k), lambda i, j: (0, 0)),
            _layer_spec((k, tn), layer, lambda i, j: (0, j)),
        ],
        out_specs=pl.BlockSpec((tm, tn), lambda i, j: (i, j)),
        scratch_shapes=[pltpu.VMEM((tm, k), BF16)],
        compiler_params=_params("parallel", "arbitrary"),
        name="mm_norm",
    )(x, g.reshape(1, k), w)


def _mm_relu2_kernel(a_ref, w_ref, o_ref):
    acc = jnp.dot(a_ref[...], w_ref[...], preferred_element_type=F32)
    o_ref[...] = jnp.square(jnp.maximum(acc, 0.0)).astype(o_ref.dtype)


def mm_relu2(a, w, layer):
    m, k = a.shape
    n = w.shape[-1]
    tm = min(MM_TM, m)
    tn = min(MLP_UP_TN, n)
    return pl.pallas_call(
        _mm_relu2_kernel,
        out_shape=jax.ShapeDtypeStruct((m, n), BF16),
        grid=(m // tm, n // tn),
        in_specs=[
            pl.BlockSpec((tm, k), lambda i, j: (i, 0)),
            _layer_spec((k, tn), layer, lambda i, j: (0, j)),
        ],
        out_specs=pl.BlockSpec((tm, tn), lambda i, j: (i, j)),
        compiler_params=_params("parallel", "arbitrary"),
        name="mm_relu2",
    )(a, w)


def _mm_res_kernel(a_ref, w_ref, r_ref, *rest, normed, single_k):
    o_ref = rest[-1]

    if single_k:
        o_ref[...] = r_ref[...] + jnp.dot(a_ref[...], w_ref[...], preferred_element_type=F32)
    else:
        @pl.when(pl.program_id(2) == 0)
        def _():
            o_ref[...] = r_ref[...]

        o_ref[...] += jnp.dot(a_ref[...], w_ref[...], preferred_element_type=F32)

    if normed:
        @pl.when(pl.program_id(2) == pl.num_programs(2) - 1)
        def _():
            o_ref[...] = _rms_rows(o_ref[...], rest[0][...])


def mm_res(a, w, layer, res, norm_gain=None):
    m, k = a.shape
    n = w.shape[-1]
    normed = norm_gain is not None
    tm = min(MM_TM // 2, m) if normed else min(MM_TM, m)
    tn = n if normed else min(MM_TN, n)
    tk = min(MM_TK // 2 if normed else MM_TK, k)
    in_specs = [
        pl.BlockSpec((tm, tk), lambda i, j, kk: (i, kk)),
        _layer_spec((tk, tn), layer, lambda i, j, kk: (kk, j)),
        pl.BlockSpec((tm, tn), lambda i, j, kk: (i, j)),
    ]
    args = [a, w, res]
    if normed:
        in_specs.append(pl.BlockSpec((1, n), lambda i, j, kk: (0, 0)))
        args.append(norm_gain.reshape(1, n))
    return pl.pallas_call(
        functools.partial(_mm_res_kernel, normed=normed, single_k=(k == tk)),
        out_shape=jax.ShapeDtypeStruct((m, n), F32),
        grid=(m // tm, n // tn, k // tk),
        in_specs=in_specs,
        out_specs=pl.BlockSpec((tm, tn), lambda i, j, kk: (i, j)),
        compiler_params=_params("parallel", "parallel", "arbitrary"),
        name="mm_res_norm" if normed else "mm_res",
    )(*args)


def _rope_half(x, cos, sin_signed):
    return x * cos + pltpu.roll(x, HEAD_DIM // 2, 1) * sin_signed


def _qkv_a_kernel(x_ref, g_ref, w_ref, qg_ref, kg_ref, cos_ref, sin_ref, qk_ref, v_ref, xn_ref):
    j = pl.program_id(1)
    n_q_tiles = A_Q_HEADS // A_KV_HEADS

    @pl.when(j == 0)
    def _():
        xn_ref[...] = _rms_rows(x_ref[...], g_ref[...]).astype(BF16)

    acc = jnp.dot(xn_ref[...], w_ref[...], preferred_element_type=F32)

    @pl.when(j <= n_q_tiles)
    def _():
        gain = jnp.where(j < n_q_tiles, qg_ref[...] * (ATTN_SCALE * LOG2E), kg_ref[...])
        cos = cos_ref[...]
        sin = sin_ref[...]
        heads = []
        for h in range(A_KV_HEADS):
            sl = slice(h * HEAD_DIM, (h + 1) * HEAD_DIM)
            heads.append(_rope_half(_rms_rows(acc[:, sl], gain), cos, sin).astype(BF16))
        qk_ref[...] = jnp.concatenate(heads, axis=1)

    @pl.when(j == n_q_tiles + 1)
    def _():
        ones = jnp.ones((v_ref.shape[0], HEAD_DIM), BF16)
        for h in range(A_KV_HEADS):
            v_ref[:, 2 * h * HEAD_DIM:(2 * h + 1) * HEAD_DIM] = acc[:, h * HEAD_DIM:(h + 1) * HEAD_DIM].astype(BF16)
            v_ref[:, (2 * h + 1) * HEAD_DIM:(2 * h + 2) * HEAD_DIM] = ones


def qkv_a(x, g, w, layer, q_gain, k_gain, cos, sin_signed, seq):
    m, k = x.shape
    nkv = A_KV_HEADS * HEAD_DIM
    tm = MM_TM
    tn = nkv
    nt = seq // tm
    n_qk_tiles = (A_Q_HEADS + A_KV_HEADS) * HEAD_DIM // tn
    tab_spec = pl.BlockSpec((tm, HEAD_DIM), lambda i, j: (i % nt, 0))
    g_spec = pl.BlockSpec((1, HEAD_DIM), lambda i, j: (0, 0))
    return pl.pallas_call(
        _qkv_a_kernel,
        out_shape=(
            jax.ShapeDtypeStruct((m, n_qk_tiles * tn), BF16),
            jax.ShapeDtypeStruct((m, 2 * nkv), BF16),
        ),
        grid=(m // tm, w.shape[-1] // tn),
        in_specs=[
            pl.BlockSpec((tm, k), lambda i, j: (i, 0)),
            pl.BlockSpec((1, k), lambda i, j: (0, 0)),
            _layer_spec((k, tn), layer, lambda i, j: (0, j)),
            g_spec,
            g_spec,
            tab_spec,
            tab_spec,
        ],
        out_specs=(
            pl.BlockSpec((tm, tn), lambda i, j: (i, jnp.minimum(j, n_qk_tiles - 1))),
            pl.BlockSpec((tm, 2 * nkv), lambda i, j: (i, 0)),
        ),
        scratch_shapes=[pltpu.VMEM((tm, k), BF16)],
        compiler_params=_params("parallel", "arbitrary"),
        name="qkv_a",
    )(x, g.reshape(1, k), w, q_gain.reshape(1, HEAD_DIM), k_gain.reshape(1, HEAD_DIM), cos, sin_signed)


def _stack_q(q_ref, qs_ref, tq):
    for g in range(A_GROUP):
        qs_ref[g * tq:(g + 1) * tq, :] = q_ref[0, :, g * HEAD_DIM:(g + 1) * HEAD_DIM]


def _unstack_o(o, o_ref, tq):
    for g in range(A_GROUP):
        o_ref[0, :, g * HEAD_DIM:(g + 1) * HEAD_DIM] = o[g * tq:(g + 1) * tq, :].astype(o_ref.dtype)


def _flash_a_bounded_kernel(q_ref, k_ref, v_ref, o_ref, qs_ref, acc_ref, *, tq, tk, nk):
    _stack_q(q_ref, qs_ref, tq)
    acc_ref[...] = jnp.zeros(acc_ref.shape, F32)

    def step(c, carry):
        start = pl.multiple_of(c * tk, tk)
        kc = k_ref[0, pl.ds(start, tk), :]
        vc = v_ref[0, pl.ds(start, tk), :]
        s = lax.dot_general(qs_ref[...], kc, (((1,), (1,)), ((), ())), preferred_element_type=F32)
        acc_ref[...] += jnp.dot(jnp.exp2(s).astype(BF16), vc, preferred_element_type=F32)
        return carry

    lax.fori_loop(0, nk, step, 0, unroll=True)
    acc = acc_ref[...]
    _unstack_o(acc[:, :HEAD_DIM] / acc[:, HEAD_DIM:], o_ref, tq)


def _flash_a_online_kernel(q_ref, k_ref, v_ref, o_ref, qs_ref, m_ref, acc_ref, *, tq, tk, nk):
    _stack_q(q_ref, qs_ref, tq)
    m_ref[...] = jnp.full(m_ref.shape, NEG_BIG, F32)
    acc_ref[...] = jnp.zeros(acc_ref.shape, F32)

    def step(c, carry):
        start = pl.multiple_of(c * tk, tk)
        kc = k_ref[0, pl.ds(start, tk), :]
        vc = v_ref[0, pl.ds(start, tk), :]
        s = lax.dot_general(qs_ref[...], kc, (((1,), (1,)), ((), ())), preferred_element_type=F32)
        m_prev = m_ref[...]
        m_new = jnp.maximum(m_prev, jnp.max(s, axis=1, keepdims=True))
        alpha = jnp.exp2(m_prev - m_new)
        p = jnp.exp2(s - m_new)
        acc_ref[...] = alpha * acc_ref[...] + jnp.dot(p.astype(BF16), vc, preferred_element_type=F32)
        m_ref[...] = m_new
        return carry

    lax.fori_loop(0, nk, step, 0)
    acc = acc_ref[...]
    _unstack_o(acc[:, :HEAD_DIM] / acc[:, HEAD_DIM:], o_ref, tq)


def _flash_a_call(qk, v, *, bounded):
    bsz, seq, _ = qk.shape
    nq = A_Q_HEADS * HEAD_DIM
    tq, tk = FLASH_TQ, FLASH_TK
    gw = A_GROUP * HEAD_DIM
    rows = A_GROUP * tq
    scratch = [pltpu.VMEM((rows, HEAD_DIM), BF16)]
    if not bounded:
        scratch.append(pltpu.VMEM((rows, 1), F32))
    scratch.append(pltpu.VMEM((rows, 2 * HEAD_DIM), F32))
    body = _flash_a_bounded_kernel if bounded else _flash_a_online_kernel
    return pl.pallas_call(
        functools.partial(body, tq=tq, tk=tk, nk=seq // tk),
        out_shape=jax.ShapeDtypeStruct((bsz, seq, nq), BF16),
        grid=(bsz, A_KV_HEADS, seq // tq),
        in_specs=[
            pl.BlockSpec((1, tq, gw), lambda b, h, i: (b, i, h)),
            pl.BlockSpec((1, seq, HEAD_DIM), lambda b, h, i: (b, 0, A_Q_HEADS + h)),
            pl.BlockSpec((1, seq, 2 * HEAD_DIM), lambda b, h, i: (b, 0, h)),
        ],
        out_specs=pl.BlockSpec((1, tq, gw), lambda b, h, i: (b, i, h)),
        scratch_shapes=scratch,
        compiler_params=_params("parallel", "parallel", "arbitrary"),
        name="flash_a_bounded" if bounded else "flash_a_online",
    )(qk, qk, v)


def flash_a(qk, v, q_gain, k_gain):
    bound = HEAD_DIM * ATTN_SCALE * jnp.max(jnp.abs(q_gain)) * jnp.max(jnp.abs(k_gain))
    return lax.cond(
        bound <= BOUNDED_SCORE_LIMIT,
        functools.partial(_flash_a_call, bounded=True),
        functools.partial(_flash_a_call, bounded=False),
        qk, v,
    )


def _qkv_b_kernel(x_ref, g_ref, w_ref, cos_ref, sin_ref, o_ref, xn_ref, *, tiles_per_which):
    j = pl.program_id(1)

    @pl.when(j == 0)
    def _():
        xn_ref[...] = _rms_rows(x_ref[...], g_ref[...]).astype(BF16)

    acc = jnp.dot(xn_ref[...], w_ref[...], preferred_element_type=F32)

    rotary = j < 2 * tiles_per_which
    scale = jnp.where(j < tiles_per_which, ATTN_SCALE * LOG2E, 1.0).astype(F32)
    cos = jnp.where(rotary, cos_ref[...] * scale, 1.0)
    sin = jnp.where(rotary, sin_ref[...] * scale, 0.0)
    for h in range(o_ref.shape[1] // HEAD_DIM):
        sl = slice(h * HEAD_DIM, (h + 1) * HEAD_DIM)
        o_ref[:, sl] = _rope_half(acc[:, sl], cos, sin).astype(BF16)


def qkv_b(x, g, w, layer, cos, sin_signed, seq):
    m, k = x.shape
    n = w.shape[-1]
    tm = MM_TM
    tn = QKV_B_TN
    assert (n // 3) % tn == 0
    nt = seq // tm
    tab_spec = pl.BlockSpec((tm, HEAD_DIM), lambda i, j: (i % nt, 0))
    return pl.pallas_call(
        functools.partial(_qkv_b_kernel, tiles_per_which=n // 3 // tn),
        out_shape=jax.ShapeDtypeStruct((m, n), BF16),
        grid=(m // tm, n // tn),
        in_specs=[
            pl.BlockSpec((tm, k), lambda i, j: (i, 0)),
            pl.BlockSpec((1, k), lambda i, j: (0, 0)),
            _layer_spec((k, tn), layer, lambda i, j: (0, j)),
            tab_spec,
            tab_spec,
        ],
        out_specs=pl.BlockSpec((tm, tn), lambda i, j: (i, j)),
        scratch_shapes=[pltpu.VMEM((tm, k), BF16)],
        compiler_params=_params("parallel", "arbitrary"),
        name="qkv_b",
    )(x, g.reshape(1, k), w, cos, sin_signed)


def _band_mask():
    rel = np.arange(2 * BAND_Q)[None, :] - np.arange(BAND_Q)[:, None]
    return np.where((rel >= 0) & (rel <= BAND_Q), 0.0, NEG_BIG).astype(np.float32)


def _band_halo(dil):
    return max(BAND_SIDE * dil, BAND_MIN_HALO)


def _rows(start, size, stride):
    return pl.ds(start, size) if stride == 1 else pl.ds(start, size, stride=stride)


def _band_b_kernel(*refs):
    ng = N_B_GROUPS
    q_refs = refs[0:ng]
    k_refs = refs[ng:4 * ng]
    v_refs = refs[4 * ng:7 * ng]
    mask_ref = refs[7 * ng]
    o_ref = refs[7 * ng + 1]
    scratch = refs[7 * ng + 2:]
    dec = [g for g, (_, dil) in enumerate(B_GROUPS) if dil > 1]
    nd = len(dec)
    qf, kf, vf = (dict(zip(dec, scratch[n * nd:(n + 1) * nd])) for n in range(3))
    og, lg, mg = (scratch[3 * nd + n * ng:3 * nd + (n + 1) * ng] for n in range(3))
    tile = BAND_TILE
    i = pl.program_id(1)

    def window(pieces, lo, h):
        hi = lo + 2 * BAND_Q
        parts = []
        if lo < 0:
            parts.append(pieces[0][0, h + lo:h, :])
        parts.append(pieces[1][0, max(lo, 0):min(hi, tile), :])
        if hi > tile:
            parts.append(pieces[2][0, 0:hi - tile, :])
        return parts[0] if len(parts) == 1 else jnp.concatenate(parts, axis=0)

    for g in dec:
        dil = B_GROUPS[g][1]
        h = _band_halo(dil)
        qf[g][...] = q_refs[g][0].astype(F32)
        for src, dst in ((k_refs, kf[g]), (v_refs, vf[g])):
            dst[0:h, :] = src[3 * g][0].astype(F32)
            dst[h:h + tile, :] = src[3 * g + 1][0].astype(F32)
            dst[h + tile:, :] = src[3 * g + 2][0].astype(F32)

    staged = [g for g in dec if B_GROUPS[g][1] > BAND_STAGE]
    base = 3 * nd + 3 * ng
    q4, k4, v4, o4, l4, m4 = (
        dict(zip(staged, scratch[base + n * len(staged):base + (n + 1) * len(staged)])) for n in range(6))
    for g in staged:
        for src, dst in ((qf[g], q4[g]), (kf[g], k4[g]), (vf[g], v4[g])):
            n = src.shape[0] // BAND_STAGE
            for a in range(BAND_STAGE):
                dst[a * n:(a + 1) * n, :] = src[pl.ds(a, n, stride=BAND_STAGE), :]

    col = lax.broadcasted_iota(jnp.int32, (1, 2 * BAND_Q), 1)
    lo_row = jnp.where((i == 0) & (col < BAND_SIDE), NEG_BIG, 0.0).astype(F32)
    hi_row = jnp.where((i == pl.num_programs(1) - 1) & (col >= 2 * BAND_Q - BAND_SIDE), NEG_BIG, 0.0).astype(F32)
    mask = mask_ref[...]
    masks = {(False, False): mask, (True, False): mask + lo_row,
             (False, True): mask + hi_row, (True, True): mask + lo_row + hi_row}
    ones = jnp.ones((2 * BAND_Q, HEAD_DIM), BF16)

    for g, (_, dil) in enumerate(B_GROUPS):
        h = _band_halo(dil)
        n_sub = tile // (BAND_Q * dil)
        for mb in range(n_sub):
            for r in range(dil):
                q_rows = _rows(dil * BAND_Q * mb + r, BAND_Q, dil)
                if dil == 1:
                    qb = q_refs[g][0, q_rows, :]
                    kb = window(k_refs[3 * g:3 * g + 3], BAND_Q * mb - BAND_SIDE, h)
                    vb = window(v_refs[3 * g:3 * g + 3], BAND_Q * mb - BAND_SIDE, h)
                elif g in staged:
                    a, b = r % BAND_STAGE, r // BAND_STAGE
                    d2 = dil // BAND_STAGE
                    nq, nk = q4[g].shape[0] // BAND_STAGE, k4[g].shape[0] // BAND_STAGE
                    q4_rows = pl.ds(a * nq + d2 * BAND_Q * mb + b, BAND_Q, stride=d2)
                    k4_rows = pl.ds(a * nk + (h + dil * (BAND_Q * mb - BAND_SIDE)) // BAND_STAGE + b, 2 * BAND_Q, stride=d2)
                    qb = q4[g][q4_rows, :].astype(BF16)
                    kb = k4[g][k4_rows, :].astype(BF16)
                    vb = v4[g][k4_rows, :].astype(BF16)
                else:
                    k_rows = _rows(h + dil * (BAND_Q * mb - BAND_SIDE) + r, 2 * BAND_Q, dil)
                    qb = qf[g][q_rows, :].astype(BF16)
                    kb = kf[g][k_rows, :].astype(BF16)
                    vb = vf[g][k_rows, :].astype(BF16)
                vb = jnp.concatenate([vb, ones], axis=1)
                s = lax.dot_general(qb, kb, (((1,), (1,)), ((), ())), preferred_element_type=F32)
                s = s + masks[(mb == 0, mb == n_sub - 1)]
                m = s.max(axis=1, keepdims=True)
                o = jnp.dot(jnp.exp2(s - m).astype(BF16), vb, preferred_element_type=F32)
                dst, rows = ((o4[g], l4[g], m4[g]), q4_rows) if g in staged else ((og[g], lg[g], mg[g]), q_rows)
                dst[0][rows, :] = o[:, :HEAD_DIM]
                dst[1][rows, :] = o[:, HEAD_DIM:]
                dst[2][rows, :] = jnp.broadcast_to(m, (BAND_Q, HEAD_DIM))
        if g in staged:
            n = tile // BAND_STAGE
            for src, out in ((o4[g], og[g]), (l4[g], lg[g]), (m4[g], mg[g])):
                for a in range(BAND_STAGE):
                    out[pl.ds(a, n, stride=BAND_STAGE), :] = src[a * n:(a + 1) * n, :]

    chunk = 2 * BAND_Q
    for c in range(tile // chunk):
        rows = slice(c * chunk, (c + 1) * chunk)
        ms = [mg[g][rows, :] for g in range(ng)]
        m = functools.reduce(jnp.maximum, ms)
        ws = [jnp.exp2(mi - m) for mi in ms]
        num = sum(ws[g] * og[g][rows, :] for g in range(ng))
        den = sum(ws[g] * lg[g][rows, :] for g in range(ng))
        o_ref[0, rows, :] = (num / den).astype(o_ref.dtype)


def band_b(qkv, mask):
    bsz, seq, _ = qkv.shape
    tile = BAND_TILE
    nh = N_B_GROUPS * B_HEADS
    assert all(win == 2 * BAND_SIDE * dil for win, dil in B_GROUPS)
    halos = [_band_halo(dil) for _, dil in B_GROUPS]
    assert all(tile % h == 0 and tile % (BAND_Q * dil) == 0 for h, (_, dil) in zip(halos, B_GROUPS))

    def col(which, g, h):
        return which * nh + g * B_HEADS + h

    def tile_spec(which, g):
        return pl.BlockSpec((1, tile, HEAD_DIM), lambda b, i, h: (b, i, col(which, g, h)))

    def halo_spec(which, g, side):
        per_tile = tile // halos[g]
        last = seq // halos[g] - 1

        def imap(b, i, h):
            blk = i * per_tile - 1 if side < 0 else (i + 1) * per_tile
            return (b, jnp.clip(blk, 0, last), col(which, g, h))
        return pl.BlockSpec((1, halos[g], HEAD_DIM), imap)

    def kv_specs(which):
        return [s for g in range(N_B_GROUPS)
                for s in (halo_spec(which, g, -1), tile_spec(which, g), halo_spec(which, g, +1))]

    in_specs = [tile_spec(0, g) for g in range(N_B_GROUPS)] + kv_specs(1) + kv_specs(2)
    in_specs += [pl.BlockSpec(mask.shape, lambda b, i, h: (0, 0))]
    n_in = len(in_specs) - 1
    dec_halos = [h for h, (_, dil) in zip(halos, B_GROUPS) if dil > 1]
    scratch = [pltpu.VMEM((tile, HEAD_DIM), F32) for _ in dec_halos]
    scratch += [pltpu.VMEM((tile + 2 * h, HEAD_DIM), F32) for _ in range(2) for h in dec_halos]
    scratch += [pltpu.VMEM((tile, HEAD_DIM), F32) for _ in range(3 * N_B_GROUPS)]
    stage_halos = [h for h, (_, dil) in zip(halos, B_GROUPS) if dil > BAND_STAGE]
    scratch += [pltpu.VMEM((tile, HEAD_DIM), F32) for _ in stage_halos]
    scratch += [pltpu.VMEM((tile + 2 * h, HEAD_DIM), F32) for _ in range(2) for h in stage_halos]
    scratch += [pltpu.VMEM((tile, HEAD_DIM), F32) for _ in range(3) for _ in stage_halos]
    return pl.pallas_call(
        _band_b_kernel,
        out_shape=jax.ShapeDtypeStruct((bsz, seq, B_HEADS * HEAD_DIM), BF16),
        grid=(bsz, seq // tile, B_HEADS),
        in_specs=in_specs,
        out_specs=pl.BlockSpec((1, tile, HEAD_DIM), lambda b, i, h: (b, i, h)),
        scratch_shapes=scratch,
        compiler_params=_params("parallel", "parallel", "arbitrary"),
        name="band_b",
    )(*([qkv] * n_in), mask)


def _cross_kernel(x_ref, g_ref, wq_ref, kv_ref, wo_ref, gn_ref, o_ref, on_ref):
    nkv = C_HEADS * HEAD_DIM
    chain = x_ref.shape[0] // CROSS_CHAINS
    for c in range(CROSS_CHAINS):
        rows = slice(c * chain, (c + 1) * chain)
        x = x_ref[rows, :]
        xn = _rms_rows(x, g_ref[...]).astype(BF16)
        q = (jnp.dot(xn, wq_ref[...], preferred_element_type=F32) * ATTN_SCALE).astype(BF16)
        outs = []
        for h in range(C_HEADS):
            sl = slice(h * HEAD_DIM, (h + 1) * HEAD_DIM)
            kh = kv_ref[0, :, sl]
            vh = kv_ref[0, :, nkv + h * HEAD_DIM:nkv + (h + 1) * HEAD_DIM]
            s = lax.dot_general(q[:, sl], kh, (((1,), (1,)), ((), ())), preferred_element_type=F32)
            p = jnp.exp(s - s.max(axis=1, keepdims=True))
            l = p.sum(axis=1, keepdims=True)
            outs.append((jnp.dot(p.astype(BF16), vh, preferred_element_type=F32) / l).astype(BF16))
        o = jnp.concatenate(outs, axis=1)
        y = x + jnp.dot(o, wo_ref[...], preferred_element_type=F32)
        o_ref[rows, :] = y
        on_ref[rows, :] = _rms_rows(y, gn_ref[...]).astype(BF16)


def cross_block(x, g, wq, kv, wo, layer, next_gain, seq):
    m, d = x.shape
    tm = CROSS_TM
    nt = seq // tm
    nkv = C_HEADS * HEAD_DIM
    mem_len = kv.shape[1]
    return pl.pallas_call(
        _cross_kernel,
        out_shape=(jax.ShapeDtypeStruct((m, d), F32), jax.ShapeDtypeStruct((m, d), BF16)),
        grid=(m // tm,),
        in_specs=[
            pl.BlockSpec((tm, d), lambda i: (i, 0)),
            pl.BlockSpec((1, d), lambda i: (0, 0)),
            _layer_spec((d, nkv), layer, lambda i: (0, 0)),
            pl.BlockSpec((1, mem_len, 2 * nkv), lambda i: (i // nt, 0, 0)),
            _layer_spec((nkv, d), layer, lambda i: (0, 0)),
            pl.BlockSpec((1, d), lambda i: (0, 0)),
        ],
        out_specs=(pl.BlockSpec((tm, d), lambda i: (i, 0)), pl.BlockSpec((tm, d), lambda i: (i, 0))),
        compiler_params=_params("parallel"),
        name="cross_block",
    )(x, g.reshape(1, d), wq, kv, wo, next_gain.reshape(1, d))


def _rope_angles(pos, dim):
    inv = ROPE_THETA ** (-jnp.arange(0, dim, 2, dtype=F32) / dim)
    ang = pos[:, None] * inv[None, :]
    return jnp.concatenate([ang, ang], axis=-1)


def _rope_tables(seq):
    rows = seq // GRID_W
    row = jnp.repeat(jnp.arange(rows, dtype=F32), GRID_W)
    col = jnp.tile(jnp.arange(GRID_W, dtype=F32), rows)
    half = HEAD_DIM // 2
    ang = jnp.concatenate([_rope_angles(row, half), _rope_angles(col, half)], axis=-1)
    low = jnp.arange(HEAD_DIM) < half
    cos_a = jnp.cos(ang)[:, A_HEAD_PERM]
    sin_a = jnp.sin(ang)[:, A_HEAD_PERM]
    sin_a = jnp.where(low, -sin_a, sin_a)
    a1 = _rope_angles(jnp.arange(seq, dtype=F32), HEAD_DIM)
    cos_b = jnp.cos(a1)
    sin_b = jnp.where(low, -jnp.sin(a1), jnp.sin(a1))
    return (cos_a, sin_a), (cos_b, sin_b)


def _permute_a_heads(a_wqkv, a_q_gain, a_k_gain):
    n_qk = (A_Q_HEADS + A_KV_HEADS) * HEAD_DIM
    cols = np.arange(a_wqkv.shape[-1])
    cols[:n_qk] = (cols[:n_qk].reshape(-1, HEAD_DIM)[:, A_HEAD_PERM]).reshape(-1)
    return a_wqkv[..., cols], a_q_gain[..., A_HEAD_PERM], a_k_gain[..., A_HEAD_PERM]


def _trunk(x, mem, wts, tabs_a, tabs_b, masks):
    bsz, seq, d = x.shape
    mem_len = mem.shape[1]
    x = x.reshape(bsz * seq, d)
    mem = mem.reshape(bsz * mem_len, d)
    depth = wts["norm_mix"].shape[0]
    for i in range(depth):
        j = i // 2
        if i % 2 == 0:
            qk, v = qkv_a(x, wts["norm_mix"][i], wts["a_wqkv"], j, wts["a_q_gain"][j], wts["a_k_gain"][j],
                          tabs_a[0], tabs_a[1], seq)
            o = flash_a(qk.reshape(bsz, seq, -1), v.reshape(bsz, seq, -1), wts["a_q_gain"][j], wts["a_k_gain"][j])
            x = mm_res(o.reshape(bsz * seq, -1), wts["a_wo"], j, x)
        else:
            qkv = qkv_b(x, wts["norm_mix"][i], wts["b_wqkv"], j, tabs_b[0], tabs_b[1], seq)
            o = band_b(qkv.reshape(bsz, seq, -1), masks)
            x = mm_res(o.reshape(bsz * seq, -1), wts["b_wo"], j, x)
        kv = mm_norm(mem, wts["norm_mem"][i], wts["c_wkv"], i)
        x, xn = cross_block(x, wts["norm_cross"][i], wts["c_wq"], kv.reshape(bsz, mem_len, -1), wts["c_wo"], i,
                            wts["norm_mlp"][i], seq)
        h = mm_relu2(xn, wts["w_up"], i)
        x = mm_res(h, wts["w_down"], i, x, norm_gain=wts["final_norm"] if i == depth - 1 else None)
    return x.reshape(bsz, seq, d)


def kernel(x_prompt, x_sample, mem_prompt, mem_sample, norm_mix, a_wqkv, a_q_gain, a_k_gain, a_wo, b_wqkv, b_wo, norm_cross, norm_mem, c_wq, c_wkv, c_wo, norm_mlp, w_up, w_down, final_norm):
    a_wqkv, a_q_gain, a_k_gain = _permute_a_heads(a_wqkv, a_q_gain, a_k_gain)
    wts = dict(
        norm_mix=norm_mix, a_wqkv=a_wqkv.astype(BF16), a_q_gain=a_q_gain, a_k_gain=a_k_gain,
        a_wo=a_wo.astype(BF16), b_wqkv=b_wqkv.astype(BF16), b_wo=b_wo.astype(BF16),
        norm_cross=norm_cross, norm_mem=norm_mem, c_wq=c_wq.astype(BF16), c_wkv=c_wkv.astype(BF16),
        c_wo=c_wo.astype(BF16), norm_mlp=norm_mlp, w_up=w_up.astype(BF16), w_down=w_down.astype(BF16),
        final_norm=final_norm,
    )
    masks = jnp.asarray(_band_mask())
    outs = []
    for x, mem in ((x_prompt, mem_prompt), (x_sample, mem_sample)):
        tabs_a, tabs_b = _rope_tables(x.shape[1])
        outs.append(_trunk(x, mem, wts, tabs_a, tabs_b, masks))
    return tuple(outs)
```
